```python
import math
import numpy as np
import jax
import jax.numpy as jnp
from jax import lax

D_MODEL = 1024
BATCH = 8
SEQ = 2048
DEPTH = 4
DEC_BATCH = 128
DEC_SEQ = 8
PAST_LEN = 2048
PAGE_SIZE = 128

HEAD_DIM = 64
N_MIX_HEADS = D_MODEL // HEAD_DIM
H_A = N_MIX_HEADS // 2
H_B = N_MIX_HEADS - H_A
H_C = N_MIX_HEADS // 2
H_D = N_MIX_HEADS - H_C
N_KV_HEADS = N_MIX_HEADS
DIFF_DIM = HEAD_DIM // 2
ROT_DIM = HEAD_DIM // 4
DIFF_ROT = DIFF_DIM // 4
IDX_HEADS = 4
IDX_DIM = 64
IDX_ROT = IDX_DIM // 4
DSA_TOPK = 256
MOBA_BLOCK = 256
MOBA_TOPK = 3
ROPE_THETA = 500000.0
D_FF = ((8 * D_MODEL // 3 + 255) // 256) * 256
MEM_TOKENS = 256
MEM_HEADS = 4
MEM_HEAD_DIM = 128
MEM_W = MEM_HEADS * MEM_HEAD_DIM
N_EVEN = (DEPTH + 1) // 2
N_ODD = DEPTH // 2
BLOCK_Q = 128
DSA_ROWS = 512
MOBA_ROWS = 64
EPS = 1e-6
EVEN_SIZES = [H_A * HEAD_DIM] * 3 + [H_B * HEAD_DIM] * 3 + [IDX_HEADS * IDX_DIM, IDX_DIM, IDX_HEADS]
EVEN_CUTS = np.cumsum(EVEN_SIZES)[:-1].tolist()
D_IN_EVEN = sum(EVEN_SIZES)
ODD_SIZES = [H_C * HEAD_DIM] * 3 + [H_D * HEAD_DIM] * 3
ODD_CUTS = np.cumsum(ODD_SIZES)[:-1].tolist()
D_IN_ODD = sum(ODD_SIZES)

kernel_name = 'hybrid_sb_dsa_diff_moba_decoder_step'

F32 = jnp.float32


def _rmsnorm(x, g):
    xf = x.astype(F32)
    y = xf * lax.rsqrt(jnp.mean(xf * xf, axis=-1, keepdims=True) + EPS)
    return (y * g.astype(F32)).astype(x.dtype)


def _rope(x, pos, rot):
    half = rot // 2
    freq = ROPE_THETA ** (-jnp.arange(half, dtype=F32) * 2.0 / rot)
    ang = pos.astype(F32)[:, None] * freq[None, :]
    ang = ang.reshape((1, pos.shape[0]) + (1,) * (x.ndim - 3) + (half,))
    cos, sin = jnp.cos(ang), jnp.sin(ang)
    xf = x.astype(F32)
    x1, x2, rest = xf[..., :half], xf[..., half:rot], xf[..., rot:]
    out = jnp.concatenate([x1 * cos - x2 * sin, x2 * cos + x1 * sin, rest], axis=-1)
    return out.astype(x.dtype)


def _swiglu(h, w_gu, w_d):
    g, u = jnp.split(h @ w_gu, 2, axis=-1)
    return (jax.nn.silu(g) * u) @ w_d


def _block(T, target):
    b = max(1, min(target, T))
    while T % b:
        b -= 1
    return b


def _sweep(fn, qs, q_pos, qb):
    T = q_pos.shape[0]
    nb = T // qb
    def split(a):
        return jnp.swapaxes(a.reshape((a.shape[0], nb, qb) + a.shape[2:]), 0, 1)
    xs = tuple(split(a) for a in qs) + (q_pos.reshape(nb, qb),)
    out = lax.map(lambda args: fn(*args), xs)
    out = jnp.swapaxes(out, 0, 1)
    return out.reshape((out.shape[0], T) + out.shape[3:])


def _stick_breaking(q, pos, k, v, k_pos):
    z = jnp.einsum('bqhd,bkhd->bhqk', q, k).astype(F32) * (HEAD_DIM ** -0.5)
    mask = k_pos[None, :] < pos[:, None]
    m = jnp.where(mask, jax.nn.log_sigmoid(-z), 0.0)
    suffix = lax.cumsum(m, axis=3, reverse=True) - m
    w = jnp.where(mask, jnp.exp(jax.nn.log_sigmoid(z) + suffix), 0.0)
    return jnp.einsum('bhqk,bkhd->bqhd', w.astype(v.dtype), v)


def _dsa(q, qi, wi, pos, k, v, ki, k_pos, ksel):
    s = jnp.einsum('bqhd,bkd->bqhk', qi, ki).astype(F32)
    score = jnp.einsum('bqhk,bqh->bqk', jax.nn.relu(s), wi.astype(F32)) * (IDX_DIM ** -0.5)
    causal = k_pos[None, :] <= pos[:, None]
    score = jnp.where(causal[None], score, -jnp.inf)
    vals, sel = lax.top_k(score, ksel)
    valid = jnp.isfinite(vals)
    bi = jnp.arange(k.shape[0])[:, None, None]
    kg = k[bi, sel]
    vg = v[bi, sel]
    logits = jnp.einsum('bqhd,bqkhd->bqhk', q, kg).astype(F32) * (HEAD_DIM ** -0.5)
    logits = jnp.where(valid[:, :, None, :], logits, -jnp.inf)
    p = jax.nn.softmax(logits, axis=-1)
    return jnp.einsum('bqhk,bqkhd->bqhd', p.astype(vg.dtype), vg)


def _diff_attn(q, pos, k, v, k_pos, lam):
    s = jnp.einsum('bqhcd,bkhcd->bchqk', q, k).astype(F32) * (DIFF_DIM ** -0.5)
    mask = k_pos[None, :] <= pos[:, None]
    a = jax.nn.softmax(jnp.where(mask, s, -jnp.inf), axis=-1)
    w = a[:, 0] - lam * a[:, 1]
    return jnp.einsum('bhqk,bkhd->bqhd', w.astype(v.dtype), v)


def _moba(q, q_pos, k, v):
    B, L, H, dh = k.shape
    n_blk = -(-L // MOBA_BLOCK)
    pad = n_blk * MOBA_BLOCK - L
    kp = jnp.pad(k, ((0, 0), (0, pad), (0, 0), (0, 0))).reshape(B, n_blk, MOBA_BLOCK, H, dh)
    vp = jnp.pad(v, ((0, 0), (0, pad), (0, 0), (0, 0))).reshape(B, n_blk, MOBA_BLOCK, H, dh)
    k_mean = jnp.mean(kp.astype(F32), axis=2)
    kbh = jnp.transpose(kp, (0, 3, 1, 2, 4))
    vbh = jnp.transpose(vp, (0, 3, 1, 2, 4))
    topb = max(1, min(MOBA_TOPK, n_blk - 1))
    bi = jnp.arange(B)[:, None, None, None]
    hi = jnp.arange(H)[None, None, :, None]
    r = jnp.arange(MOBA_BLOCK, dtype=jnp.int32)

    def blockfn(qq, pos):
        qb = qq.shape[1]
        gate = jnp.einsum('bqhd,bnhd->bqhn', qq.astype(F32), k_mean)
        own = pos // MOBA_BLOCK
        past_ok = jnp.arange(n_blk)[None, :] < own[:, None]
        gate = jnp.where(past_ok[None, :, None, :], gate, -jnp.inf)
        vals, sel = lax.top_k(gate, topb)
        own_b = jnp.broadcast_to(own[None, :, None, None], (B, qb, H, 1)).astype(jnp.int32)
        blocks = jnp.concatenate([sel.astype(jnp.int32), own_b], axis=-1)
        sel_ok = jnp.concatenate([jnp.isfinite(vals), jnp.ones((B, qb, H, 1), dtype=bool)], axis=-1)
        kg = kbh[bi, hi, blocks]
        vg = vbh[bi, hi, blocks]
        kpos = blocks[..., None] * MOBA_BLOCK + r
        mask = sel_ok[..., None] & (kpos <= pos[None, :, None, None, None])
        logits = jnp.einsum('bqhd,bqhsrd->bqhsr', qq, kg).astype(F32) * (HEAD_DIM ** -0.5)
        logits = jnp.where(mask, logits, -jnp.inf)
        S = blocks.shape[-1]
        p = jax.nn.softmax(logits.reshape(B, qb, H, S * MOBA_BLOCK), axis=-1).reshape(logits.shape)
        return jnp.einsum('bqhsr,bqhsrd->bqhd', p.astype(vg.dtype), vg)

    return _sweep(blockfn, (q,), q_pos, _block(q.shape[1], max(1, MOBA_ROWS // B)))


def _even_mixer(h, q_pos, past, w_in, g_qk):
    B, T, _ = h.shape
    qa, ka, va, qb, kb, vb, qi, ki, wi = jnp.split(h @ w_in, EVEN_CUTS, axis=-1)
    qa = qa.reshape(B, T, H_A, HEAD_DIM)
    ka = ka.reshape(B, T, H_A, HEAD_DIM)
    va = va.reshape(B, T, H_A, HEAD_DIM)
    qb = _rope(_rmsnorm(qb.reshape(B, T, H_B, HEAD_DIM), g_qk[0]), q_pos, ROT_DIM)
    kb = _rope(_rmsnorm(kb.reshape(B, T, H_B, HEAD_DIM), g_qk[1]), q_pos, ROT_DIM)
    vb = vb.reshape(B, T, H_B, HEAD_DIM)
    qi = _rope(qi.reshape(B, T, IDX_HEADS, IDX_DIM), q_pos, IDX_ROT)
    ki = _rope(ki, q_pos, IDX_ROT)
    new_k = jnp.concatenate([ka, kb], axis=2)
    new_v = jnp.concatenate([va, vb], axis=2)
    if past is None:
        k_all, v_all, ki_all = new_k, new_v, ki
    else:
        k_all = jnp.concatenate([past[0], new_k], axis=1)
        v_all = jnp.concatenate([past[1], new_v], axis=1)
        ki_all = jnp.concatenate([past[2], ki], axis=1)
    L = k_all.shape[1]
    k_pos = jnp.arange(L, dtype=jnp.int32)
    ka_all, kb_all = k_all[:, :, :H_A], k_all[:, :, H_A:]
    va_all, vb_all = v_all[:, :, :H_A], v_all[:, :, H_A:]
    oa = _sweep(lambda q, pos: _stick_breaking(q, pos, ka_all, va_all, k_pos),
                (qa,), q_pos, _block(T, BLOCK_Q))
    ksel = min(DSA_TOPK, L // 4)
    ob = _sweep(lambda q, qi_, wi_, pos: _dsa(q, qi_, wi_, pos, kb_all, vb_all, ki_all, k_pos, ksel),
                (qb, qi, wi), q_pos, _block(T, max(1, DSA_ROWS // B)))
    o = jnp.concatenate([oa, ob], axis=2).reshape(B, T, N_MIX_HEADS * HEAD_DIM)
    return o, new_k, new_v, ki


def _odd_mixer(h, q_pos, past, w_in, g_qk_c, g_qk_d, g_sub, lam_p, lam_init):
    B, T, _ = h.shape
    qc, kc, vc, qd, kd, vd = jnp.split(h @ w_in, ODD_CUTS, axis=-1)
    def diff_heads(a, g):
        a = a.reshape(B, T, H_C, 2, DIFF_DIM)
        return _rope(_rmsnorm(a, g.reshape(2, DIFF_DIM)), q_pos, DIFF_ROT)
    qc = diff_heads(qc, g_qk_c[0])
    kc = diff_heads(kc, g_qk_c[1])
    vc = vc.reshape(B, T, H_C, HEAD_DIM)
    qd = _rope(_rmsnorm(qd.reshape(B, T, H_D, HEAD_DIM), g_qk_d[0]), q_pos, ROT_DIM)
    kd = _rope(_rmsnorm(kd.reshape(B, T, H_D, HEAD_DIM), g_qk_d[1]), q_pos, ROT_DIM)
    vd = vd.reshape(B, T, H_D, HEAD_DIM)
    new_k = jnp.concatenate([kc.reshape(B, T, H_C, HEAD_DIM), kd], axis=2)
    new_v = jnp.concatenate([vc, vd], axis=2)
    if past is None:
        k_all, v_all = new_k, new_v
    else:
        k_all = jnp.concatenate([past[0], new_k], axis=1)
        v_all = jnp.concatenate([past[1], new_v], axis=1)
    L = k_all.shape[1]
    k_pos = jnp.arange(L, dtype=jnp.int32)
    kc_all = k_all[:, :, :H_C].reshape(B, L, H_C, 2, DIFF_DIM)
    vc_all = v_all[:, :, :H_C]
    kd_all, vd_all = k_all[:, :, H_C:], v_all[:, :, H_C:]
    lp = lam_p.astype(F32)
    lam = jnp.exp(jnp.sum(lp[0] * lp[1])) - jnp.exp(jnp.sum(lp[2] * lp[3])) + lam_init
    oc = _sweep(lambda q, pos: _diff_attn(q, pos, kc_all, vc_all, k_pos, lam),
                (qc,), q_pos, _block(T, BLOCK_Q))
    oc = _rmsnorm(oc, g_sub) * (1.0 - lam_init)
    od = _moba(qd, q_pos, kd_all, vd_all)
    o = jnp.concatenate([oc, od], axis=2).reshape(B, T, N_MIX_HEADS * HEAD_DIM)
    return o, new_k, new_v


def _mem_kv(mem, g, w_kv, g_k):
    B, M, _ = mem.shape
    k, v = jnp.split(_rmsnorm(mem, g) @ w_kv, 2, axis=-1)
    k = _rmsnorm(k.reshape(B, M, MEM_HEADS, MEM_HEAD_DIM), g_k)
    return k, v.reshape(B, M, MEM_HEADS, MEM_HEAD_DIM)


def _cross_attn(h, mk, mv, w_q, g_q, w_o):
    B, T, _ = h.shape
    q = _rmsnorm((h @ w_q).reshape(B, T, MEM_HEADS, MEM_HEAD_DIM), g_q)
    s = jnp.einsum('bthd,bmhd->bhtm', q, mk).astype(F32) * (MEM_HEAD_DIM ** -0.5)
    p = jax.nn.softmax(s, axis=-1)
    o = jnp.einsum('bhtm,bmhd->bthd', p.astype(mv.dtype), mv)
    return o.reshape(B, T, MEM_W) @ w_o


def _trunk(x, pos0, get_past, get_mem, W):
    B, T, _ = x.shape
    q_pos = jnp.arange(T, dtype=jnp.int32) + pos0
    ks, vs, idxs, mks, mvs = [], [], [], [], []
    for l in range(DEPTH):
        g = W['g_norm'][l]
        x = x + 0.5 * _swiglu(_rmsnorm(x, g[0]), W['w_ffn1_gu'][l], W['w_ffn1_d'][l])
        h = _rmsnorm(x, g[1])
        past = get_past(l)
        if l % 2 == 0:
            o, nk, nv, ni = _even_mixer(h, q_pos, past, W['w_in_even'][l // 2], W['g_qk_b'][l // 2])
            idxs.append(ni)
        else:
            lam_init = 0.8 - 0.6 * math.exp(-0.3 * l)
            o, nk, nv = _odd_mixer(h, q_pos, past, W['w_in_odd'][l // 2], W['g_qk_c'][l // 2],
                                   W['g_qk_d'][l // 2], W['g_sub_c'][l // 2], W['lambda_c'][l // 2], lam_init)
        x = x + o @ W['w_out'][l]
        mk, mv = get_mem(l)
        x = x + _cross_attn(_rmsnorm(x, g[2]), mk, mv, W['w_mem_q'][l], W['g_mem_qk'][l, 0], W['w_mem_o'][l])
        x = x + 0.5 * _swiglu(_rmsnorm(x, g[3]), W['w_ffn2_gu'][l], W['w_ffn2_d'][l])
        ks.append(nk)
        vs.append(nv)
        mks.append(mk)
        mvs.append(mv)
    return x, jnp.stack(ks, axis=1), jnp.stack(vs, axis=1), jnp.stack(idxs, axis=1), mks, mvs


def setup_inputs(seed: int = 0) -> dict:
    key = jax.random.key(seed)
    kit = iter(jax.random.split(key, 32))
    n_pages = PAST_LEN // PAGE_SIZE
    n_phys = (DEC_BATCH * n_pages * 5) // 4
    def nrm(shape, s=1.0):
        return jax.random.normal(next(kit), shape, F32) * s
    def gain(shape):
        return 1.0 + 0.02 * jax.random.normal(next(kit), shape, F32)
    d = D_MODEL
    inp = {}
    inp['x_prompt'] = nrm((BATCH, SEQ, d))
    inp['x_sample'] = nrm((DEC_BATCH, DEC_SEQ, d))
    inp['mem_prompt'] = nrm((BATCH, MEM_TOKENS, d))
    inp['cache_k'] = nrm((n_phys, DEPTH, PAGE_SIZE, N_KV_HEADS, HEAD_DIM))
    inp['cache_v'] = nrm((n_phys, DEPTH, PAGE_SIZE, N_KV_HEADS, HEAD_DIM))
    inp['cache_idx_k'] = nrm((n_phys, N_EVEN, PAGE_SIZE, IDX_DIM))
    inp['cache_mem_k'] = nrm((DEC_BATCH, DEPTH, MEM_TOKENS, MEM_HEADS, MEM_HEAD_DIM))
    inp['cache_mem_v'] = nrm((DEC_BATCH, DEPTH, MEM_TOKENS, MEM_HEADS, MEM_HEAD_DIM))
    perm = jax.random.permutation(next(kit), n_phys)[:DEC_BATCH * n_pages]
    inp['page_table'] = perm.reshape(DEC_BATCH, n_pages).astype(jnp.int32)
    inp['g_norm'] = gain((DEPTH, 5, d))
    inp['w_ffn1_gu'] = nrm((DEPTH, d, 2 * D_FF), d ** -0.5)
    inp['w_ffn1_d'] = nrm((DEPTH, D_FF, d), D_FF ** -0.5)
    inp['w_ffn2_gu'] = nrm((DEPTH, d, 2 * D_FF), d ** -0.5)
    inp['w_ffn2_d'] = nrm((DEPTH, D_FF, d), D_FF ** -0.5)
    inp['w_in_even'] = nrm((N_EVEN, d, D_IN_EVEN), d ** -0.5)
    inp['g_qk_b'] = gain((N_EVEN, 2, HEAD_DIM))
    inp['w_in_odd'] = nrm((N_ODD, d, D_IN_ODD), d ** -0.5)
    inp['g_qk_c'] = gain((N_ODD, 2, HEAD_DIM))
    inp['g_qk_d'] = gain((N_ODD, 2, HEAD_DIM))
    inp['g_sub_c'] = gain((N_ODD, HEAD_DIM))
    inp['lambda_c'] = nrm((N_ODD, 4, DIFF_DIM), 0.1)
    inp['w_out'] = nrm((DEPTH, N_MIX_HEADS * HEAD_DIM, d), (N_MIX_HEADS * HEAD_DIM) ** -0.5)
    inp['w_mem_q'] = nrm((DEPTH, d, MEM_W), d ** -0.5)
    inp['w_mem_kv'] = nrm((DEPTH, d, 2 * MEM_W), d ** -0.5)
    inp['w_mem_o'] = nrm((DEPTH, MEM_W, d), MEM_W ** -0.5)
    inp['g_mem_qk'] = gain((DEPTH, 2, MEM_HEAD_DIM))
    return inp


def reference(x_prompt, x_sample, mem_prompt, cache_k, cache_v, cache_idx_k, cache_mem_k, cache_mem_v,
              page_table, g_norm, w_ffn1_gu, w_ffn1_d, w_ffn2_gu, w_ffn2_d, w_in_even, g_qk_b, w_in_odd,
              g_qk_c, g_qk_d, g_sub_c, lambda_c, w_out, w_mem_q, w_mem_kv, w_mem_o, g_mem_qk):
    W = dict(g_norm=g_norm, w_ffn1_gu=w_ffn1_gu, w_ffn1_d=w_ffn1_d, w_ffn2_gu=w_ffn2_gu, w_ffn2_d=w_ffn2_d,
             w_in_even=w_in_even, g_qk_b=g_qk_b, w_in_odd=w_in_odd, g_qk_c=g_qk_c, g_qk_d=g_qk_d,
             g_sub_c=g_sub_c, lambda_c=lambda_c, w_out=w_out, w_mem_q=w_mem_q, w_mem_kv=w_mem_kv,
             w_mem_o=w_mem_o, g_mem_qk=g_mem_qk)

    def prompt_mem(l):
        return _mem_kv(mem_prompt, g_norm[l, 4], w_mem_kv[l], g_mem_qk[l, 1])
    y_prompt, k_prompt, v_prompt, idx_k_prompt, mks, mvs = _trunk(
        x_prompt, 0, lambda l: None, prompt_mem, W)
    mem_k_prompt = jnp.stack(mks, axis=1)
    mem_v_prompt = jnp.stack(mvs, axis=1)

    db, n_pages = page_table.shape
    past_len = n_pages * PAGE_SIZE
    def sample_past(l):
        pk = cache_k[page_table, l].reshape(db, past_len, N_KV_HEADS, HEAD_DIM)
        pv = cache_v[page_table, l].reshape(db, past_len, N_KV_HEADS, HEAD_DIM)
        if l % 2 == 0:
            pi = cache_idx_k[page_table, l // 2].reshape(db, past_len, IDX_DIM)
            return (pk, pv, pi)
        return (pk, pv)
    def sample_mem(l):
        return cache_mem_k[:, l], cache_mem_v[:, l]
    y_sample, k_sample, v_sample, idx_k_sample, _, _ = _trunk(
        x_sample, past_len, sample_past, sample_mem, W)

    return (y_prompt, y_sample, k_prompt, v_prompt, idx_k_prompt, mem_k_prompt, mem_v_prompt,
            k_sample, v_sample, idx_k_sample)
```

```python
import functools
import math

import jax
import jax.numpy as jnp
from jax import lax
from jax.experimental import pallas as pl
from jax.experimental.pallas import tpu as pltpu

F32 = jnp.float32
BF16 = jnp.bfloat16
I32 = jnp.int32

EPS = 1e-6
HEAD_DIM = 64
DIFF_DIM = 32
ROT_DIM = 16
DIFF_ROT = 8
IDX_HEADS = 4
IDX_DIM = 64
DSA_TOPK = 256
MOBA_BLOCK = 256
MOBA_TOPK = 3
ROPE_THETA = 500000.0
MEM_HEADS = 4
MEM_HEAD_DIM = 128

LANES = 128
SEC = 512
NEG = -1e30
VMEM_LIMIT = 56 * 1024 * 1024


def _cparams(*sem):
    return pltpu.CompilerParams(dimension_semantics=sem, vmem_limit_bytes=VMEM_LIMIT)


def _pick(n, target):
    t = min(n, target)
    while t > 8 and (n % t or t % 8):
        t -= 1
    assert n % t == 0, (n, target)
    return t


def _dot(a, b):
    return jnp.dot(a, b, preferred_element_type=F32)


def _dot_nt(a, b):
    return lax.dot_general(a, b, (((1,), (1,)), ((), ())), preferred_element_type=F32)


def _split(x):
    hi = x.astype(BF16)
    lo = (x - hi.astype(F32)).astype(BF16)
    return hi, lo


def _dot_hilo(x, w_bf16):
    hi, lo = _split(x)
    return _dot(hi, w_bf16) + _dot(lo, w_bf16)


def _dot_nt_hilo2(a, b):
    ah, al = _split(a)
    bh, bl = _split(b)
    return _dot_nt(ah, bh) + _dot_nt(ah, bl) + _dot_nt(al, bh)


def _dot_hilo2(a, b):
    ah, al = _split(a)
    bh, bl = _split(b)
    return _dot(ah, bh) + _dot(ah, bl) + _dot(al, bh)


def _iota(shape, dim):
    return lax.broadcasted_iota(I32, shape, dim)


def _rms_rows(x, g):
    return x * lax.rsqrt(jnp.mean(x * x, axis=-1, keepdims=True) + EPS) * g


def _ffn_kernel(x_ref, g_ref, wg_ref, wu_ref, wd_ref, o_ref, h_scr, acc_scr):
    f = pl.program_id(1)

    @pl.when(f == 0)
    def _():
        h_scr[...] = _rms_rows(x_ref[...], g_ref[...]).astype(BF16)
        acc_scr[...] = jnp.zeros_like(acc_scr)

    h = h_scr[...]
    g = _dot(h, wg_ref[...])
    u = _dot(h, wu_ref[...])
    a = (g * (1.0 / (1.0 + jnp.exp(-g)))) * u
    acc_scr[...] += _dot(a.astype(BF16), wd_ref[...])

    @pl.when(f == pl.num_programs(1) - 1)
    def _():
        o_ref[...] = x_ref[...] + 0.5 * acc_scr[...]


def _ffn(x, g, w_gu, w_d):
    M, D = x.shape
    F = w_d.shape[0]
    tm = _pick(M, 1024)
    tf = _pick(F, 256)
    nf = F // tf
    return pl.pallas_call(
        _ffn_kernel,
        grid=(M // tm, nf),
        in_specs=[
            pl.BlockSpec((tm, D), lambda i, f: (i, 0)),
            pl.BlockSpec((1, D), lambda i, f: (0, 0)),
            pl.BlockSpec((D, tf), lambda i, f: (0, f)),
            pl.BlockSpec((D, tf), lambda i, f: (0, nf + f)),
            pl.BlockSpec((tf, D), lambda i, f: (f, 0)),
        ],
        out_specs=pl.BlockSpec((tm, D), lambda i, f: (i, 0)),
        out_shape=jax.ShapeDtypeStruct((M, D), F32),
        scratch_shapes=[pltpu.VMEM((tm, D), BF16), pltpu.VMEM((tm, D), F32)],
        compiler_params=_cparams("parallel", "arbitrary"),
        name="ffn",
    )(x, g.reshape(1, D), w_gu, w_gu, w_d)


def _group_ones(gs):
    r = jnp.arange(SEC) // gs
    return (r[:, None] == r[None, :]).astype(BF16)


def _rope_tables(pos, gs, rot):
    half = rot // 2
    freq = ROPE_THETA ** (-jnp.arange(half, dtype=F32) * 2.0 / rot)
    ang = pos.astype(F32)[:, None] * freq[None, :]
    cos, sin = jnp.cos(ang), jnp.sin(ang)
    d = jnp.arange(LANES) % gs
    c_full = cos[:, d % half]
    s_full = sin[:, d % half]
    C = jnp.where(d[None, :] < rot, c_full, 1.0)
    S = jnp.where(d[None, :] < half, -s_full, jnp.where(d[None, :] < rot, s_full, 0.0))
    return C, S


def _proj_kernel(*refs, cfgs, ropes, group_sizes, tn):
    x_ref, g_ref, w_ref, gain_ref, rmask_ref = refs[:5]
    n_tab = 2 * len(ropes)
    tab_refs = refs[5:5 + n_tab]
    bd_refs = refs[5 + n_tab:5 + n_tab + len(group_sizes)]
    o_ref, h_scr = refs[5 + n_tab + len(group_sizes):]
    j = pl.program_id(1)

    @pl.when(j == 0)
    def _():
        h_scr[...] = _rms_rows(x_ref[...], g_ref[...]).astype(BF16)

    y = _dot(h_scr[...], w_ref[...])
    tm = y.shape[0]

    def epilogue(y, cfg):
        gs, rope = cfg
        if gs:
            bd = bd_refs[group_sizes.index(gs)][...]
            ssq = _dot_hilo(y * y, bd)
            y = y * lax.rsqrt(ssq * (1.0 / gs) + EPS) * gain_ref[...]
        if rope < 0:
            o_ref[...] = y
            return
        period, half = ropes[rope]
        C = tab_refs[2 * rope][...]
        S = tab_refs[2 * rope + 1][...]
        first = (_iota((tm, LANES), 1) % period) < half
        for c in range(tn // LANES):
            sl = slice(c * LANES, (c + 1) * LANES)
            yc = y[:, sl]
            on = rmask_ref[:, sl] > 0.0
            partner = jnp.where(first, pltpu.roll(yc, LANES - half, 1), pltpu.roll(yc, half, 1))
            o_ref[:, sl] = yc * jnp.where(on, C, 1.0) + partner * jnp.where(on, S, 0.0)

    for cfg in sorted(set(cfgs)):
        cond = None
        for s, c in enumerate(cfgs):
            if c == cfg:
                cond = (j == s) if cond is None else (cond | (j == s))
        pl.when(cond)(functools.partial(epilogue, y, cfg))


def _proj(x, gnorm, w, gain, rmask, cfgs, tabs, ropes):
    M, D = x.shape
    N = w.shape[1]
    nsec = N // SEC
    assert nsec == len(cfgs)
    tm = _pick(M, 512)
    group_sizes = tuple(sorted({c[0] for c in cfgs if c[0]}))
    bds = [_group_ones(gs) for gs in group_sizes]
    kern = functools.partial(_proj_kernel, cfgs=tuple(cfgs), ropes=tuple(ropes),
                             group_sizes=group_sizes, tn=SEC)
    in_specs = [
        pl.BlockSpec((tm, D), lambda i, j: (i, 0)),
        pl.BlockSpec((1, D), lambda i, j: (0, 0)),
        pl.BlockSpec((D, SEC), lambda i, j: (0, j)),
        pl.BlockSpec((1, SEC), lambda i, j: (0, j)),
        pl.BlockSpec((1, SEC), lambda i, j: (0, j)),
    ]
    in_specs += [pl.BlockSpec((tm, LANES), lambda i, j: (i, 0)) for _ in tabs]
    in_specs += [pl.BlockSpec((SEC, SEC), lambda i, j: (0, 0)) for _ in bds]
    return pl.pallas_call(
        kern,
        grid=(M // tm, nsec),
        in_specs=in_specs,
        out_specs=pl.BlockSpec((tm, SEC), lambda i, j: (i, j)),
        out_shape=jax.ShapeDtypeStruct((M, N), F32),
        scratch_shapes=[pltpu.VMEM((tm, D), BF16)],
        compiler_params=_cparams("parallel", "arbitrary"),
        name="proj",
    )(x, gnorm.reshape(1, D), w, gain, rmask, *tabs, *bds)


def _matres_kernel(a_ref, w_ref, x_ref, o_ref):
    o_ref[...] = x_ref[...] + _dot(a_ref[...].astype(BF16), w_ref[...])


def _matres(a, w, x):
    M, K = a.shape
    N = w.shape[1]
    tm = _pick(M, 1024)
    return pl.pallas_call(
        _matres_kernel,
        grid=(M // tm,),
        in_specs=[
            pl.BlockSpec((tm, K), lambda i: (i, 0)),
            pl.BlockSpec((K, N), lambda i: (0, 0)),
            pl.BlockSpec((tm, N), lambda i: (i, 0)),
        ],
        out_specs=pl.BlockSpec((tm, N), lambda i: (i, 0)),
        out_shape=jax.ShapeDtypeStruct((M, N), F32),
        compiler_params=_cparams("parallel"),
        name="matres",
    )(a, w, x)


def _stack_masked(q, n_groups, width):
    t, L = q.shape
    lane_grp = _iota((t, L), 1) // width
    return jnp.concatenate([jnp.where(lane_grp == r, q, 0.0) for r in range(n_groups)], axis=0)


def _log_sigmoid(z):
    return jnp.minimum(z, 0.0) - jnp.log1p(jnp.exp(-jnp.abs(z)))


def _topk_select(score, valid, ksel):
    R, L = score.shape
    score = jnp.where(valid, score, -jnp.inf)
    bits = lax.bitcast_convert_type(score, I32)
    key = jnp.where(bits < 0, bits ^ jnp.int32(0x7FFFFFFF), bits)
    kf = jnp.float32(ksel)

    def count_ge(t):
        return jnp.sum((key >= t).astype(F32), axis=1, keepdims=True)

    int_min = jnp.int32(-2 ** 31)
    thr = jnp.where(count_ge(jnp.zeros((R, 1), I32)) >= kf, jnp.int32(0), int_min)

    def body(it, thr):
        cand = thr | lax.shift_left(jnp.int32(1), 30 - it)
        return jnp.where(count_ge(cand) >= kf, cand, thr)

    thr = lax.fori_loop(0, 31, body, thr)
    need = kf - jnp.sum((key > thr).astype(F32), axis=1, keepdims=True)
    validf = jnp.where(valid, 1.0, 0.0)
    incl = (_iota((LANES, LANES), 0) <= _iota((LANES, LANES), 1)).astype(BF16)
    run = jnp.zeros((R, 1), F32)
    outs = []
    for c in range(L // LANES):
        sl = slice(c * LANES, (c + 1) * LANES)
        key_c = key[:, sl]
        eqf = jnp.where(key_c == thr, 1.0, 0.0)
        rank = _dot(eqf.astype(BF16), incl) + run
        run = run + jnp.sum(eqf, axis=1, keepdims=True)
        take = jnp.where(key_c > thr, 1.0, jnp.where(rank <= need, eqf, 0.0))
        outs.append(take * validf[:, sl])
    return jnp.concatenate(outs, axis=1)


def _top_blocks(gate, ok, topb):
    R, NB = gate.shape
    n = _iota((R, NB), 1)
    g = jnp.where(ok, gate, -jnp.inf)
    sel = jnp.zeros((R, NB), F32)
    for _ in range(topb):
        mx = jnp.max(g, axis=1, keepdims=True)
        idx = jnp.min(jnp.where(g == mx, n, NB), axis=1, keepdims=True)
        pick = n == jnp.where(mx > -jnp.inf, idx, -1)
        sel = jnp.where(pick, 1.0, sel)
        g = jnp.where(pick, -jnp.inf, g)
    return sel


def _pair_out(a, tq):
    lane = _iota((tq, LANES), 1)
    return jnp.where(lane < HEAD_DIM, a[:tq], a[tq:])


def _sb_prompt_kernel(q_ref, k_ref, v_ref, o_ref, acc_scr, *, tq):
    i = pl.program_id(2)
    tk = tq
    q2 = _stack_masked(q_ref[...], 2, HEAD_DIM).astype(BF16)
    upper = (_iota((tk, tk), 0) > _iota((tk, tk), 1)).astype(BF16)
    row = _iota((2 * tq, tk), 0)
    qpos = i * tq + jnp.where(row >= tq, row - tq, row)
    col = _iota((2 * tq, tk), 1)
    acc_scr[...] = jnp.zeros_like(acc_scr)

    def body(s, carry):
        kb = i - s
        off = pl.multiple_of(kb * tk, tk)
        k = k_ref[pl.ds(off, tk), :].astype(BF16)
        v = v_ref[pl.ds(off, tk), :].astype(BF16)
        z = _dot_nt(q2, k) * (HEAD_DIM ** -0.5)
        mask = (kb * tk + col) < qpos
        ls = _log_sigmoid(z)
        m = jnp.where(mask, ls - z, 0.0)
        suffix = _dot_hilo(m, upper) + carry
        w = jnp.where(mask, jnp.exp(ls + suffix), 0.0)
        acc_scr[...] += _dot(w.astype(BF16), v)
        return carry + jnp.sum(m, axis=1, keepdims=True)

    lax.fori_loop(0, i + 1, body, jnp.zeros((2 * tq, 1), F32))
    o_ref[...] = _pair_out(acc_scr[...], tq)


def _sb_prompt(Y, B, T, qc, kc, vc):
    tq = _pick(T, 256)
    nq = T // tq
    npair = SEC // LANES
    cpb = SEC // LANES
    return pl.pallas_call(
        functools.partial(_sb_prompt_kernel, tq=tq),
        grid=(B, npair, nq),
        in_specs=[
            pl.BlockSpec((tq, LANES), lambda b, p, i: (b * nq + i, qc * cpb + p)),
            pl.BlockSpec((T, LANES), lambda b, p, i: (b, kc * cpb + p)),
            pl.BlockSpec((T, LANES), lambda b, p, i: (b, vc * cpb + p)),
        ],
        out_specs=pl.BlockSpec((tq, LANES), lambda b, p, i: (b * nq + i, p)),
        out_shape=jax.ShapeDtypeStruct((B * T, SEC), F32),
        scratch_shapes=[pltpu.VMEM((2 * tq, LANES), F32)],
        compiler_params=_cparams("parallel", "parallel", "arbitrary"),
        name="sb_prompt",
    )(Y, Y, Y)


def _idx_scores(iq, ki_bf16, transposed=False):
    score = None
    w_off = IDX_HEADS * IDX_DIM + IDX_DIM
    for h in range(IDX_HEADS):
        qh = iq[:, h * IDX_DIM:(h + 1) * IDX_DIM].astype(BF16)
        s = _dot(qh, ki_bf16) if transposed else _dot_nt(qh, ki_bf16)
        term = jnp.maximum(s, 0.0) * iq[:, w_off + h:w_off + h + 1]
        score = term if score is None else score + term
    return score * (IDX_DIM ** -0.5)


def _dsa_prompt_kernel(q_ref, iq_ref, k_ref, v_ref, ik_ref, o_ref, *, tq, T, ksel):
    i = pl.program_id(1)
    ki = ik_ref[:, IDX_HEADS * IDX_DIM:IDX_HEADS * IDX_DIM + IDX_DIM].astype(BF16)
    score = _idx_scores(iq_ref[...], ki)
    qpos = i * tq + _iota((tq, T), 0)
    causal = _iota((tq, T), 1) <= qpos
    sel = _topk_select(score, causal, ksel)
    sel2 = jnp.concatenate([sel, sel], axis=0) > 0.0
    for p in range(SEC // LANES):
        sl = slice(p * LANES, (p + 1) * LANES)
        q2 = _stack_masked(q_ref[:, sl], 2, HEAD_DIM).astype(BF16)
        logits = _dot_nt(q2, k_ref[:, sl].astype(BF16)) * (HEAD_DIM ** -0.5)
        lg = jnp.where(sel2, logits, NEG)
        mx = jnp.max(lg, axis=1, keepdims=True)
        pe = jnp.where(sel2, jnp.exp(lg - mx), 0.0)
        l = jnp.sum(pe, axis=1, keepdims=True)
        o2 = _dot(pe.astype(BF16), v_ref[:, sl].astype(BF16)) / l
        o_ref[:, sl] = _pair_out(o2, tq)


def _dsa_prompt(Y, B, T, qc, kc, vc, ic):
    tq = _pick(T, 128)
    nq = T // tq
    ksel = min(DSA_TOPK, T // 4)
    return pl.pallas_call(
        functools.partial(_dsa_prompt_kernel, tq=tq, T=T, ksel=ksel),
        grid=(B, nq),
        in_specs=[
            pl.BlockSpec((tq, SEC), lambda b, i: (b * nq + i, qc)),
            pl.BlockSpec((tq, SEC), lambda b, i: (b * nq + i, ic)),
            pl.BlockSpec((T, SEC), lambda b, i: (b, kc)),
            pl.BlockSpec((T, SEC), lambda b, i: (b, vc)),
            pl.BlockSpec((T, SEC), lambda b, i: (b, ic)),
        ],
        out_specs=pl.BlockSpec((tq, SEC), lambda b, i: (b * nq + i, 0)),
        out_shape=jax.ShapeDtypeStruct((B * T, SEC), F32),
        compiler_params=_cparams("parallel", "arbitrary"),
        name="dsa_prompt",
    )(Y, Y, Y, Y, Y)


def _online_step(s, mask, v, m_scr, l_scr, acc_scr):
    s = jnp.where(mask, s, NEG)
    m_old = m_scr[...]
    m_new = jnp.maximum(m_old, jnp.max(s, axis=1, keepdims=True))
    alpha = jnp.exp(m_old - m_new)
    p = jnp.where(mask, jnp.exp(s - m_new), 0.0)
    l_scr[...] = alpha * l_scr[...] + jnp.sum(p, axis=1, keepdims=True)
    acc_scr[...] = alpha * acc_scr[...] + _dot(p.astype(BF16), v)
    m_scr[...] = m_new


def _online_init(m_scr, l_scr, acc_scr):
    m_scr[...] = jnp.full_like(m_scr, NEG)
    l_scr[...] = jnp.zeros_like(l_scr)
    acc_scr[...] = jnp.zeros_like(acc_scr)


def _diff_prompt_kernel(lam_ref, q_ref, k_ref, v_ref, gs_ref, o_ref, m_scr, l_scr, acc_scr, *, tq, tk, out_scale):
    i = pl.program_id(2)
    q4 = _stack_masked(q_ref[...], 4, DIFF_DIM).astype(BF16)
    row = _iota((4 * tq, tk), 0)
    qpos = i * tq + row % tq
    col = _iota((4 * tq, tk), 1)
    _online_init(m_scr, l_scr, acc_scr)

    def body(kb, carry):
        off = pl.multiple_of(kb * tk, tk)
        k = k_ref[pl.ds(off, tk), :].astype(BF16)
        v = v_ref[pl.ds(off, tk), :].astype(BF16)
        s = _dot_nt(q4, k) * (DIFF_DIM ** -0.5)
        _online_step(s, (kb * tk + col) <= qpos, v, m_scr, l_scr, acc_scr)
        return carry

    lax.fori_loop(0, ((i + 1) * tq + tk - 1) // tk, body, 0)
    a = acc_scr[...] / l_scr[...]
    lam = lam_ref[...]
    lane = _iota((tq, LANES), 1)
    low = lane < HEAD_DIM
    o = jnp.where(low, a[:tq] - lam * a[tq:2 * tq], a[2 * tq:3 * tq] - lam * a[3 * tq:])
    o2 = o * o
    s0 = jnp.sum(jnp.where(low, o2, 0.0), axis=1, keepdims=True)
    s1 = jnp.sum(jnp.where(low, 0.0, o2), axis=1, keepdims=True)
    ssq = jnp.where(low, s0, s1)
    o_ref[...] = o * lax.rsqrt(ssq * (1.0 / HEAD_DIM) + EPS) * gs_ref[...] * out_scale


def _diff_prompt(Y, B, T, qc, kc, vc, lam, gsub, lam_init):
    tq = _pick(T, 128)
    tk = _pick(T, 256)
    nq = T // tq
    cpb = SEC // LANES
    return pl.pallas_call(
        functools.partial(_diff_prompt_kernel, tq=tq, tk=tk, out_scale=1.0 - lam_init),
        grid=(B, cpb, nq),
        in_specs=[
            pl.BlockSpec((1, 1), lambda b, p, i: (0, 0)),
            pl.BlockSpec((tq, LANES), lambda b, p, i: (b * nq + i, qc * cpb + p)),
            pl.BlockSpec((T, LANES), lambda b, p, i: (b, kc * cpb + p)),
            pl.BlockSpec((T, LANES), lambda b, p, i: (b, vc * cpb + p)),
            pl.BlockSpec((1, LANES), lambda b, p, i: (0, 0)),
        ],
        out_specs=pl.BlockSpec((tq, LANES), lambda b, p, i: (b * nq + i, p)),
        out_shape=jax.ShapeDtypeStruct((B * T, SEC), F32),
        scratch_shapes=[pltpu.VMEM((4 * tq, 1), F32), pltpu.VMEM((4 * tq, 1), F32),
                        pltpu.VMEM((4 * tq, LANES), F32)],
        compiler_params=_cparams("parallel", "parallel", "arbitrary"),
        name="diff_prompt",
    )(lam, Y, Y, Y, jnp.tile(gsub, 2).reshape(1, LANES))


def _kmean_kernel(k_ref, o_ref):
    n = pl.program_id(1)

    @pl.when(n == 0)
    def _():
        o_ref[...] = jnp.zeros_like(o_ref)

    o_ref[0, pl.ds(n, 1), :] = jnp.sum(k_ref[...], axis=0, keepdims=True) * (1.0 / MOBA_BLOCK)


def _kmean(Y, B, T, kc, nbp):
    n_blk = T // MOBA_BLOCK
    return pl.pallas_call(
        _kmean_kernel,
        grid=(B, n_blk),
        in_specs=[pl.BlockSpec((MOBA_BLOCK, SEC), lambda b, n: (b * n_blk + n, kc))],
        out_specs=pl.BlockSpec((1, nbp, SEC), lambda b, n: (b, 0, 0)),
        out_shape=jax.ShapeDtypeStruct((B, nbp, SEC), F32),
        compiler_params=_cparams("parallel", "arbitrary"),
        name="kmean",
    )(Y)


def _moba_prompt_kernel(q_ref, km_ref, k_ref, v_ref, o_ref, m_scr, l_scr, acc_scr, *, tq, topb):
    i = pl.program_id(2)
    tk = tq
    q2f = _stack_masked(q_ref[...], 2, HEAD_DIM)
    q2 = q2f.astype(BF16)
    gate = _dot_nt_hilo2(q2f, km_ref[0])
    nb = gate.shape[1]
    n = _iota((2 * tq, nb), 1)
    sel = _top_blocks(gate, n < i, topb)
    row = _iota((2 * tq, tk), 0)
    qpos = i * tq + row % tq
    col = _iota((2 * tq, tk), 1)
    _online_init(m_scr, l_scr, acc_scr)

    def body(kb, carry):
        off = pl.multiple_of(kb * tk, tk)
        k = k_ref[pl.ds(off, tk), :].astype(BF16)
        v = v_ref[pl.ds(off, tk), :].astype(BF16)
        s = _dot_nt(q2, k) * (HEAD_DIM ** -0.5)
        picked = jnp.sum(jnp.where(n == kb, sel, 0.0), axis=1, keepdims=True)
        causal = jnp.where((kb * tk + col) <= qpos, 1.0, 0.0)
        mask = jnp.where(kb == i, causal, picked) > 0.0
        _online_step(s, mask, v, m_scr, l_scr, acc_scr)
        return carry

    lax.fori_loop(0, i + 1, body, 0)
    o_ref[...] = _pair_out(acc_scr[...] / l_scr[...], tq)


def _moba_prompt(Y, B, T, qc, kc, vc):
    assert T % MOBA_BLOCK == 0
    n_blk = T // MOBA_BLOCK
    nbp = -(-n_blk // LANES) * LANES
    topb = max(1, min(MOBA_TOPK, n_blk - 1))
    km = _kmean(Y, B, T, kc, nbp)
    tq = MOBA_BLOCK
    nq = T // tq
    cpb = SEC // LANES
    return pl.pallas_call(
        functools.partial(_moba_prompt_kernel, tq=tq, topb=topb),
        grid=(B, cpb, nq),
        in_specs=[
            pl.BlockSpec((tq, LANES), lambda b, p, i: (b * nq + i, qc * cpb + p)),
            pl.BlockSpec((1, nbp, LANES), lambda b, p, i: (b, 0, p)),
            pl.BlockSpec((T, LANES), lambda b, p, i: (b, kc * cpb + p)),
            pl.BlockSpec((T, LANES), lambda b, p, i: (b, vc * cpb + p)),
        ],
        out_specs=pl.BlockSpec((tq, LANES), lambda b, p, i: (b * nq + i, p)),
        out_shape=jax.ShapeDtypeStruct((B * T, SEC), F32),
        scratch_shapes=[pltpu.VMEM((2 * tq, 1), F32), pltpu.VMEM((2 * tq, 1), F32),
                        pltpu.VMEM((2 * tq, LANES), F32)],
        compiler_params=_cparams("parallel", "parallel", "arbitrary"),
        name="moba_prompt",
    )(Y, km, Y, Y)


def _cross_kernel(q_ref, mk_ref, mv_ref, o_ref):
    for h in range(MEM_HEADS):
        sl = slice(h * MEM_HEAD_DIM, (h + 1) * MEM_HEAD_DIM)
        s = _dot_nt(q_ref[:, sl].astype(BF16), mk_ref[:, sl].astype(BF16)) * (MEM_HEAD_DIM ** -0.5)
        mx = jnp.max(s, axis=1, keepdims=True)
        p = jnp.exp(s - mx)
        l = jnp.sum(p, axis=1, keepdims=True)
        o_ref[:, sl] = _dot(p.astype(BF16), mv_ref[:, sl].astype(BF16)) / l


def _cross(q, row0, B, Tq, mk, mv, mt, k_map, v_map):
    tq = _pick(Tq, 256)
    nq = Tq // tq
    r0 = row0 // tq
    assert row0 % tq == 0
    return pl.pallas_call(
        _cross_kernel,
        grid=(B, nq),
        in_specs=[
            pl.BlockSpec((tq, SEC), lambda b, i: (r0 + b * nq + i, 0)),
            pl.BlockSpec((mt, SEC), k_map),
            pl.BlockSpec((mt, SEC), v_map),
        ],
        out_specs=pl.BlockSpec((tq, SEC), lambda b, i: (b * nq + i, 0)),
        out_shape=jax.ShapeDtypeStruct((B * Tq, SEC), F32),
        compiler_params=_cparams("parallel", "arbitrary"),
        name="cross",
    )(q, mk, mv)


def _pad_rows(x, rows):
    return jnp.concatenate([x, jnp.zeros((rows - x.shape[0], x.shape[1]), x.dtype)], axis=0)


def _dec_positions(R, Ts, page, n_pages, past_len):
    col = _iota((R, page), 1)
    qpos = past_len + _iota((R, page), 0) % Ts
    kpos = [c * page + col for c in range(n_pages)] + [past_len + col]
    exists = [None] * n_pages + [col < Ts]
    return qpos, kpos, exists


def _diag_out(acc, n_heads, Ts, step=1):
    head = _iota((Ts, SEC), 1) // HEAD_DIM
    out = jnp.zeros((Ts, SEC), F32)
    for h in range(n_heads):
        r = step * h * Ts
        out = jnp.where(head == h, acc[r:r + Ts], out)
    return out


class _PagedKV:
    def __init__(self, kn_ref, vn_ref, kp, vp, page):
        self.kp, self.vp, self.page, self.n = kp, vp, page, len(kp)
        self.kn = _pad_rows(kn_ref[...], page).astype(BF16)
        self.vn = _pad_rows(vn_ref[...], page).astype(BF16)

    def kt(self, c):
        return self.kp[c][...].reshape(SEC, self.page)

    def scores(self, q_bf16, c):
        if c < self.n:
            return _dot(q_bf16, self.kt(c).astype(BF16))
        return _dot_nt(q_bf16, self.kn)

    def pv(self, p, c):
        if c < self.n:
            return _dot_nt(p.astype(BF16), self.vp[c][...].reshape(SEC, self.page).astype(BF16))
        return _dot(p.astype(BF16), self.vn)


def _softmax_chunks(logits, masks, kv):
    lg = [jnp.where(m, s, NEG) for s, m in zip(logits, masks)]
    mx = functools.reduce(jnp.maximum, [jnp.max(x, axis=1, keepdims=True) for x in lg])
    acc, l = None, None
    for c, (x, m) in enumerate(zip(lg, masks)):
        p = jnp.where(m, jnp.exp(x - mx), 0.0)
        ps = jnp.sum(p, axis=1, keepdims=True)
        pv = kv.pv(p, c)
        acc = pv if acc is None else acc + pv
        l = ps if l is None else l + ps
    return acc / l


def _sb_dec_kernel(pt_ref, q_ref, kn_ref, vn_ref, *rest, n_pages, Ts, page, past_len):
    kp, vp, o_ref = rest[:n_pages], rest[n_pages:2 * n_pages], rest[2 * n_pages]
    H = SEC // HEAD_DIM
    R = H * Ts
    qbd = _stack_masked(q_ref[...], H, HEAD_DIM).astype(BF16)
    kv = _PagedKV(kn_ref, vn_ref, kp, vp, page)
    qpos, kpos, exists = _dec_positions(R, Ts, page, n_pages, past_len)
    upper = (_iota((page, page), 0) > _iota((page, page), 1)).astype(BF16)
    ls, ms, masks = [], [], []
    for c in range(n_pages + 1):
        z = kv.scores(qbd, c) * (HEAD_DIM ** -0.5)
        mask = kpos[c] < qpos
        if exists[c] is not None:
            mask = mask & exists[c]
        l_c = _log_sigmoid(z)
        ls.append(l_c)
        ms.append(jnp.where(mask, l_c - z, 0.0))
        masks.append(mask)
    run = jnp.zeros((R, 1), F32)
    acc = jnp.zeros((R, SEC), F32)
    for c in range(n_pages, -1, -1):
        suffix = _dot_hilo(ms[c], upper) + run
        run = run + jnp.sum(ms[c], axis=1, keepdims=True)
        w = jnp.where(masks[c], jnp.exp(ls[c] + suffix), 0.0)
        acc = acc + kv.pv(w, c)
    o_ref[...] = _diag_out(acc, H, Ts)


def _dsa_dec_kernel(pt_ref, q_ref, iq_ref, kn_ref, vn_ref, *rest, n_pages, Ts, page, past_len, ksel):
    kp, vp = rest[:n_pages], rest[n_pages:2 * n_pages]
    ip, o_ref = rest[2 * n_pages:3 * n_pages], rest[3 * n_pages]
    H = SEC // HEAD_DIM
    iq = iq_ref[...]
    ki_off = IDX_HEADS * IDX_DIM
    chunks = [_idx_scores(iq, r[...].astype(BF16), transposed=True) for r in ip]
    chunks.append(_idx_scores(iq, _pad_rows(iq[:, ki_off:ki_off + IDX_DIM], page).astype(BF16)))
    score = jnp.concatenate(chunks, axis=1)
    kcol = _iota(score.shape, 1)
    valid = (kcol <= past_len + _iota(score.shape, 0)) & (kcol < past_len + Ts)
    sel = _topk_select(score, valid, ksel)
    qbd = _stack_masked(q_ref[...], H, HEAD_DIM).astype(BF16)
    kv = _PagedKV(kn_ref, vn_ref, kp, vp, page)
    logits, masks = [], []
    for c in range(n_pages + 1):
        logits.append(kv.scores(qbd, c) * (HEAD_DIM ** -0.5))
        sel_c = sel[:, c * page:(c + 1) * page]
        masks.append(jnp.concatenate([sel_c] * H, axis=0) > 0.0)
    o_ref[...] = _diag_out(_softmax_chunks(logits, masks, kv), H, Ts)


def _diff_dec_kernel(pt_ref, lam_ref, q_ref, kn_ref, vn_ref, gs_ref, bd_ref, *rest,
                     n_pages, Ts, page, past_len, out_scale):
    kp, vp, o_ref = rest[:n_pages], rest[n_pages:2 * n_pages], rest[2 * n_pages]
    G = SEC // DIFF_DIM
    R = G * Ts
    qbd = _stack_masked(q_ref[...], G, DIFF_DIM).astype(BF16)
    kv = _PagedKV(kn_ref, vn_ref, kp, vp, page)
    qpos, kpos, exists = _dec_positions(R, Ts, page, n_pages, past_len)
    logits, masks = [], []
    for c in range(n_pages + 1):
        logits.append(kv.scores(qbd, c) * (DIFF_DIM ** -0.5))
        mask = kpos[c] <= qpos
        masks.append(mask if exists[c] is None else mask & exists[c])
    a = _softmax_chunks(logits, masks, kv)
    lam = lam_ref[...]
    d = jnp.concatenate([a[(2 * h) * Ts:(2 * h + 1) * Ts] - lam * a[(2 * h + 1) * Ts:(2 * h + 2) * Ts]
                         for h in range(G // 2)], axis=0)
    o = _diag_out(d, G // 2, Ts)
    ssq = _dot_hilo(o * o, bd_ref[...])
    o_ref[...] = o * lax.rsqrt(ssq * (1.0 / HEAD_DIM) + EPS) * gs_ref[...] * out_scale


def _moba_dec_kernel(pt_ref, q_ref, kn_ref, vn_ref, *rest, n_pages, Ts, page, past_len, topb):
    kp, vp, o_ref = rest[:n_pages], rest[n_pages:2 * n_pages], rest[2 * n_pages]
    H = SEC // HEAD_DIM
    R = H * Ts
    ppb = MOBA_BLOCK // page
    own = past_len // MOBA_BLOCK
    qbdf = _stack_masked(q_ref[...], H, HEAD_DIM)
    qbd = qbdf.astype(BF16)
    kv = _PagedKV(kn_ref, vn_ref, kp, vp, page)
    nb_lane = _iota((SEC, LANES), 1)
    kmean_t = jnp.zeros((SEC, LANES), F32)
    for n in range(own):
        blk = functools.reduce(jnp.add, [kv.kt(c) for c in range(n * ppb, (n + 1) * ppb)])
        kmean_t = jnp.where(nb_lane == n, jnp.sum(blk, axis=1, keepdims=True) * (1.0 / MOBA_BLOCK), kmean_t)
    gate = _dot_hilo2(qbdf, kmean_t)
    sel = _top_blocks(gate, _iota((R, LANES), 1) < own, topb)
    qpos, kpos, exists = _dec_positions(R, Ts, page, n_pages, past_len)
    logits, masks = [], []
    for c in range(n_pages + 1):
        logits.append(kv.scores(qbd, c) * (HEAD_DIM ** -0.5))
        if c < n_pages:
            b = c // ppb
            masks.append(jnp.broadcast_to(sel[:, b:b + 1], (R, page)) > 0.0)
        else:
            masks.append((kpos[c] <= qpos) & exists[c])
    o_ref[...] = _diag_out(_softmax_chunks(logits, masks, kv), H, Ts)


def _dec_call(kern, name, page_table, Y, row0, Ts, cache_k, cache_v, layer, half, q_cols, extra=(),
              idx_cache=None, idx_layer=0):
    Bs, n_pages = page_table.shape
    page = cache_k.shape[4]
    hpb = SEC // HEAD_DIM
    rb = row0 // Ts

    def ysec(col):
        return pl.BlockSpec((Ts, SEC), lambda b, pt, col=col: (rb + b, col))

    def pspec(p):
        return pl.BlockSpec((None, None, hpb, HEAD_DIM, page),
                            lambda b, pt, p=p: (pt[b, p], layer, half, 0, 0))

    def ispec(p):
        return pl.BlockSpec((None, None, IDX_DIM, page), lambda b, pt, p=p: (pt[b, p], idx_layer, 0, 0))

    in_specs = []
    ops = []
    lead = [e for e in extra if e[0] == "lead"]
    tail = [e for e in extra if e[0] == "tail"]
    for _, arr, spec in lead:
        ops.append(arr)
        in_specs.append(spec)
    for col in q_cols:
        ops.append(Y)
        in_specs.append(ysec(col))
    for _, arr, spec in tail:
        ops.append(arr)
        in_specs.append(spec)
    for p in range(n_pages):
        ops.append(cache_k)
        in_specs.append(pspec(p))
    for p in range(n_pages):
        ops.append(cache_v)
        in_specs.append(pspec(p))
    if idx_cache is not None:
        for p in range(n_pages):
            ops.append(idx_cache)
            in_specs.append(ispec(p))
    return pl.pallas_call(
        kern,
        grid_spec=pltpu.PrefetchScalarGridSpec(
            num_scalar_prefetch=1,
            grid=(Bs,),
            in_specs=in_specs,
            out_specs=pl.BlockSpec((Ts, SEC), lambda b, pt: (b, 0)),
        ),
        out_shape=jax.ShapeDtypeStruct((Bs * Ts, SEC), F32),
        compiler_params=_cparams("arbitrary"),
        name=name,
    )(page_table, *ops)


def kernel(x_prompt, x_sample, mem_prompt, cache_k, cache_v, cache_idx_k, cache_mem_k, cache_mem_v, page_table, g_norm, w_ffn1_gu, w_ffn1_d, w_ffn2_gu, w_ffn2_d, w_in_even, g_qk_b, w_in_odd, g_qk_c, g_qk_d, g_sub_c, lambda_c, w_out, w_mem_q, w_mem_kv, w_mem_o, g_mem_qk):
    Bp, Tp, D = x_prompt.shape
    Bs, Ts, _ = x_sample.shape
    depth = g_norm.shape[0]
    n_phys, _, page, n_kv, hd = cache_k.shape
    n_pages = page_table.shape[1]
    past_len = n_pages * page
    mt = mem_prompt.shape[1]
    Mp, Ms = Bp * Tp, Bs * Ts
    assert D == 2 * SEC and n_kv * hd == D and hd == HEAD_DIM
    assert Mp % Ts == 0 and past_len % MOBA_BLOCK == 0 and MOBA_BLOCK % page == 0 and Ts <= page

    x = jnp.concatenate([x_prompt.reshape(Mp, D), x_sample.reshape(Ms, D)], axis=0)
    mem = mem_prompt.reshape(Bp * mt, D)
    pos = jnp.concatenate([jnp.tile(jnp.arange(Tp, dtype=I32), Bp),
                           jnp.tile(jnp.arange(Ts, dtype=I32) + past_len, Bs)])
    tab_a = _rope_tables(pos, HEAD_DIM, ROT_DIM)
    tab_b = _rope_tables(pos, DIFF_DIM, DIFF_ROT)
    ck = jnp.transpose(cache_k, (0, 1, 3, 4, 2))
    cv = jnp.transpose(cache_v, (0, 1, 3, 4, 2))
    cik = jnp.transpose(cache_idx_k, (0, 1, 3, 2))
    cmk = cache_mem_k.reshape(Bs * depth * mt, SEC)
    cmv = cache_mem_v.reshape(Bs * depth * mt, SEC)
    ones = jnp.ones((SEC,), F32)
    zeros = jnp.zeros((SEC,), F32)
    idx_w = IDX_HEADS * IDX_DIM + IDX_DIM + IDX_HEADS
    idx_on = (jnp.arange(SEC) < IDX_HEADS * IDX_DIM + IDX_DIM).astype(F32)
    ksel_s = min(DSA_TOPK, (past_len + Ts) // 4)
    topb_s = max(1, min(MOBA_TOPK, -(-(past_len + Ts) // MOBA_BLOCK) - 1))
    bd64 = _group_ones(HEAD_DIM)

    Ys, mkvs = [], []
    for l in range(depth):
        g = g_norm[l]
        x = _ffn(x, g[0], w_ffn1_gu[l].astype(BF16), w_ffn1_d[l].astype(BF16))
        l2 = l // 2
        if l % 2 == 0:
            w = w_in_even[l2]
            wcols = [w[:, 0:512], w[:, 1536:2048], w[:, 512:1024], w[:, 2048:2560], w[:, 1024:1536],
                     w[:, 2560:3072], jnp.pad(w[:, 3072:3072 + idx_w], ((0, 0), (0, SEC - idx_w)))]
            cfgs = [(0, -1), (HEAD_DIM, 0), (0, -1), (HEAD_DIM, 0), (0, -1), (0, -1), (0, 0)]
            gains = [ones, jnp.tile(g_qk_b[l2, 0], 8), ones, jnp.tile(g_qk_b[l2, 1], 8), ones, ones, ones]
            rmask = [zeros, ones, zeros, ones, zeros, zeros, idx_on]
            tabs, ropes = list(tab_a), [(HEAD_DIM, ROT_DIM // 2)]
        else:
            w = w_in_odd[l2]
            wcols = [w[:, 0:512], w[:, 1536:2048], w[:, 512:1024], w[:, 2048:2560], w[:, 1024:1536],
                     w[:, 2560:3072]]
            cfgs = [(DIFF_DIM, 1), (HEAD_DIM, 0), (DIFF_DIM, 1), (HEAD_DIM, 0), (0, -1), (0, -1)]
            gains = [jnp.tile(g_qk_c[l2, 0], 8), jnp.tile(g_qk_d[l2, 0], 8), jnp.tile(g_qk_c[l2, 1], 8),
                     jnp.tile(g_qk_d[l2, 1], 8), ones, ones]
            rmask = [ones, ones, ones, ones, zeros, zeros]
            tabs = list(tab_a) + list(tab_b)
            ropes = [(HEAD_DIM, ROT_DIM // 2), (DIFF_DIM, DIFF_ROT // 2)]
        Y = _proj(x, g[1], jnp.concatenate(wcols, axis=1).astype(BF16),
                  jnp.concatenate(gains).reshape(1, -1), jnp.concatenate(rmask).reshape(1, -1),
                  cfgs, tabs, ropes)
        Ys.append(Y)

        if l % 2 == 0:
            oa_p = _sb_prompt(Y, Bp, Tp, 0, 2, 4)
            ob_p = _dsa_prompt(Y, Bp, Tp, 1, 3, 5, 6)
            oa_s = _dec_call(
                functools.partial(_sb_dec_kernel, n_pages=n_pages, Ts=Ts, page=page, past_len=past_len),
                "sb_dec", page_table, Y, Mp, Ts, ck, cv, l, 0, [0, 2, 4])
            ob_s = _dec_call(
                functools.partial(_dsa_dec_kernel, n_pages=n_pages, Ts=Ts, page=page, past_len=past_len,
                                  ksel=ksel_s),
                "dsa_dec", page_table, Y, Mp, Ts, ck, cv, l, 1, [1, 6, 3, 5],
                idx_cache=cik, idx_layer=l2)
        else:
            lam_init = 0.8 - 0.6 * math.exp(-0.3 * l)
            lp = lambda_c[l2].astype(F32)
            lam = (jnp.exp(jnp.sum(lp[0] * lp[1])) - jnp.exp(jnp.sum(lp[2] * lp[3])) + lam_init).reshape(1, 1)
            oa_p = _diff_prompt(Y, Bp, Tp, 0, 2, 4, lam, g_sub_c[l2], lam_init)
            ob_p = _moba_prompt(Y, Bp, Tp, 1, 3, 5)
            one = pl.BlockSpec((1, 1), lambda b, pt: (0, 0))
            oa_s = _dec_call(
                functools.partial(_diff_dec_kernel, n_pages=n_pages, Ts=Ts, page=page, past_len=past_len,
                                  out_scale=1.0 - lam_init),
                "diff_dec", page_table, Y, Mp, Ts, ck, cv, l, 0, [0, 2, 4],
                extra=[("lead", lam, one),
                       ("tail", jnp.tile(g_sub_c[l2], 8).reshape(1, SEC),
                        pl.BlockSpec((1, SEC), lambda b, pt: (0, 0))),
                       ("tail", bd64, pl.BlockSpec((SEC, SEC), lambda b, pt: (0, 0)))])
            ob_s = _dec_call(
                functools.partial(_moba_dec_kernel, n_pages=n_pages, Ts=Ts, page=page, past_len=past_len,
                                  topb=topb_s),
                "moba_dec", page_table, Y, Mp, Ts, ck, cv, l, 1, [1, 3, 5])
        o = jnp.concatenate([jnp.concatenate([oa_p, ob_p], axis=1),
                             jnp.concatenate([oa_s, ob_s], axis=1)], axis=0)
        x = _matres(o, w_out[l].astype(BF16), x)

        g_mq = jnp.tile(g_mem_qk[l, 0], MEM_HEADS).reshape(1, SEC)
        qm = _proj(x, g[2], w_mem_q[l].astype(BF16), g_mq, jnp.zeros((1, SEC), F32),
                   [(MEM_HEAD_DIM, -1)], [], [])
        g_mk = jnp.concatenate([jnp.tile(g_mem_qk[l, 1], MEM_HEADS), ones]).reshape(1, 2 * SEC)
        mkv = _proj(mem, g[4], w_mem_kv[l].astype(BF16), g_mk, jnp.zeros((1, 2 * SEC), F32),
                    [(MEM_HEAD_DIM, -1), (0, -1)], [], [])
        mkvs.append(mkv)
        oc_p = _cross(qm, 0, Bp, Tp, mkv, mkv, mt, lambda b, i: (b, 0), lambda b, i: (b, 1))
        oc_s = _cross(qm, Mp, Bs, Ts, cmk, cmv, mt,
                      lambda b, i, l=l: (b * depth + l, 0), lambda b, i, l=l: (b * depth + l, 0))
        x = _matres(jnp.concatenate([oc_p, oc_s], axis=0), w_mem_o[l].astype(BF16), x)
        x = _ffn(x, g[3], w_ffn2_gu[l].astype(BF16), w_ffn2_d[l].astype(BF16))

    def stack(rows0, rows1, col0, col1, shape, arrs):
        return jnp.stack([a[rows0:rows1, col0:col1].reshape(shape) for a in arrs], axis=1)

    ki0 = 6 * SEC + IDX_HEADS * IDX_DIM
    y_prompt = x[:Mp].reshape(Bp, Tp, D)
    y_sample = x[Mp:].reshape(Bs, Ts, D)
    k_prompt = stack(0, Mp, 2 * SEC, 4 * SEC, (Bp, Tp, n_kv, hd), Ys)
    v_prompt = stack(0, Mp, 4 * SEC, 6 * SEC, (Bp, Tp, n_kv, hd), Ys)
    idx_k_prompt = stack(0, Mp, ki0, ki0 + IDX_DIM, (Bp, Tp, IDX_DIM), Ys[0::2])
    mem_k_prompt = stack(0, Bp * mt, 0, SEC, (Bp, mt, MEM_HEADS, MEM_HEAD_DIM), mkvs)
    mem_v_prompt = stack(0, Bp * mt, SEC, 2 * SEC, (Bp, mt, MEM_HEADS, MEM_HEAD_DIM), mkvs)
    k_sample = stack(Mp, Mp + Ms, 2 * SEC, 4 * SEC, (Bs, Ts, n_kv, hd), Ys)
    v_sample = stack(Mp, Mp + Ms, 4 * SEC, 6 * SEC, (Bs, Ts, n_kv, hd), Ys)
    idx_k_sample = stack(Mp, Mp + Ms, ki0, ki0 + IDX_DIM, (Bs, Ts, IDX_DIM), Ys[0::2])
    return (y_prompt, y_sample, k_prompt, v_prompt, idx_k_prompt, mem_k_prompt, mem_v_prompt,
            k_sample, v_sample, idx_k_sample)
```

```python
import functools
import math

import jax
import jax.numpy as jnp
from jax import lax
from jax.experimental import pallas as pl
from jax.experimental.pallas import tpu as pltpu

F32 = jnp.float32
BF16 = jnp.bfloat16
I32 = jnp.int32

EPS = 1e-6
HEAD_DIM = 64
DIFF_DIM = 32
ROT_DIM = 16
DIFF_ROT = 8
IDX_HEADS = 4
IDX_DIM = 64
DSA_TOPK = 256
MOBA_BLOCK = 256
MOBA_TOPK = 3
ROPE_THETA = 500000.0
MEM_HEADS = 4
MEM_HEAD_DIM = 128

LANES = 128
SEC = 512
NEG = -1e30
VMEM_LIMIT = 56 * 1024 * 1024


def _cparams(*sem):
    return pltpu.CompilerParams(dimension_semantics=sem, vmem_limit_bytes=VMEM_LIMIT)


def _pick(n, target):
    t = min(n, target)
    while t > 8 and (n % t or t % 8):
        t -= 1
    assert n % t == 0, (n, target)
    return t


def _dot(a, b):
    return jnp.dot(a, b, preferred_element_type=F32)


def _dot_nt(a, b):
    return lax.dot_general(a, b, (((1,), (1,)), ((), ())), preferred_element_type=F32)


def _split(x):
    hi = x.astype(BF16)
    lo = (x - hi.astype(F32)).astype(BF16)
    return hi, lo


def _dot_hilo(x, w_bf16):
    hi, lo = _split(x)
    return _dot(hi, w_bf16) + _dot(lo, w_bf16)


def _dot_nt_hilo2(a, b):
    ah, al = _split(a)
    bh, bl = _split(b)
    return _dot_nt(ah, bh) + _dot_nt(ah, bl) + _dot_nt(al, bh)


def _dot_hilo2(a, b):
    ah, al = _split(a)
    bh, bl = _split(b)
    return _dot(ah, bh) + _dot(ah, bl) + _dot(al, bh)


def _iota(shape, dim):
    return lax.broadcasted_iota(I32, shape, dim)


def _rms_rows(x, g):
    return x * lax.rsqrt(jnp.mean(x * x, axis=-1, keepdims=True) + EPS) * g


def _ffn_kernel(x_ref, g_ref, wg_ref, wu_ref, wd_ref, o_ref, h_scr, acc_scr):
    f = pl.program_id(1)

    @pl.when(f == 0)
    def _():
        h_scr[...] = _rms_rows(x_ref[...], g_ref[...]).astype(BF16)
        acc_scr[...] = jnp.zeros_like(acc_scr)

    h = h_scr[...]
    g = _dot(h, wg_ref[...])
    u = _dot(h, wu_ref[...])
    a = (g * (1.0 / (1.0 + jnp.exp(-g)))) * u
    acc_scr[...] += _dot(a.astype(BF16), wd_ref[...])

    @pl.when(f == pl.num_programs(1) - 1)
    def _():
        o_ref[...] = x_ref[...] + 0.5 * acc_scr[...]


def _ffn(x, g, w_gu, w_d):
    M, D = x.shape
    F = w_d.shape[0]
    tm = _pick(M, 1024)
    tf = _pick(F, 256)
    nf = F // tf
    return pl.pallas_call(
        _ffn_kernel,
        grid=(M // tm, nf),
        in_specs=[
            pl.BlockSpec((tm, D), lambda i, f: (i, 0)),
            pl.BlockSpec((1, D), lambda i, f: (0, 0)),
            pl.BlockSpec((D, tf), lambda i, f: (0, f)),
            pl.BlockSpec((D, tf), lambda i, f: (0, nf + f)),
            pl.BlockSpec((tf, D), lambda i, f: (f, 0)),
        ],
        out_specs=pl.BlockSpec((tm, D), lambda i, f: (i, 0)),
        out_shape=jax.ShapeDtypeStruct((M, D), F32),
        scratch_shapes=[pltpu.VMEM((tm, D), BF16), pltpu.VMEM((tm, D), F32)],
        compiler_params=_cparams("parallel", "arbitrary"),
        name="ffn",
    )(x, g.reshape(1, D), w_gu, w_gu, w_d)


def _group_ones(gs):
    r = jnp.arange(SEC) // gs
    return (r[:, None] == r[None, :]).astype(BF16)


def _rope_tables(pos, gs, rot):
    half = rot // 2
    freq = ROPE_THETA ** (-jnp.arange(half, dtype=F32) * 2.0 / rot)
    ang = pos.astype(F32)[:, None] * freq[None, :]
    cos, sin = jnp.cos(ang), jnp.sin(ang)
    d = jnp.arange(LANES) % gs
    c_full = cos[:, d % half]
    s_full = sin[:, d % half]
    C = jnp.where(d[None, :] < rot, c_full, 1.0)
    S = jnp.where(d[None, :] < half, -s_full, jnp.where(d[None, :] < rot, s_full, 0.0))
    return C, S


def _proj_kernel(*refs, cfgs, ropes, group_sizes, tn):
    x_ref, g_ref, w_ref, gain_ref, rmask_ref = refs[:5]
    n_tab = 2 * len(ropes)
    tab_refs = refs[5:5 + n_tab]
    bd_refs = refs[5 + n_tab:5 + n_tab + len(group_sizes)]
    o_ref, h_scr = refs[5 + n_tab + len(group_sizes):]
    j = pl.program_id(1)

    @pl.when(j == 0)
    def _():
        h_scr[...] = _rms_rows(x_ref[...], g_ref[...]).astype(BF16)

    y = _dot(h_scr[...], w_ref[...])
    tm = y.shape[0]

    def epilogue(y, cfg):
        gs, rope = cfg
        if gs:
            bd = bd_refs[group_sizes.index(gs)][...]
            ssq = _dot_hilo(y * y, bd)
            y = y * lax.rsqrt(ssq * (1.0 / gs) + EPS) * gain_ref[...]
        if rope < 0:
            o_ref[...] = y
            return
        period, half = ropes[rope]
        C = tab_refs[2 * rope][...]
        S = tab_refs[2 * rope + 1][...]
        first = (_iota((tm, LANES), 1) % period) < half
        for c in range(tn // LANES):
            sl = slice(c * LANES, (c + 1) * LANES)
            yc = y[:, sl]
            on = rmask_ref[:, sl] > 0.0
            partner = jnp.where(first, pltpu.roll(yc, LANES - half, 1), pltpu.roll(yc, half, 1))
            o_ref[:, sl] = yc * jnp.where(on, C, 1.0) + partner * jnp.where(on, S, 0.0)

    for cfg in sorted(set(cfgs)):
        cond = None
        for s, c in enumerate(cfgs):
            if c == cfg:
                cond = (j == s) if cond is None else (cond | (j == s))
        pl.when(cond)(functools.partial(epilogue, y, cfg))


def _proj(x, gnorm, w, gain, rmask, cfgs, tabs, ropes):
    M, D = x.shape
    N = w.shape[1]
    nsec = N // SEC
    assert nsec == len(cfgs)
    tm = _pick(M, 512)
    group_sizes = tuple(sorted({c[0] for c in cfgs if c[0]}))
    bds = [_group_ones(gs) for gs in group_sizes]
    kern = functools.partial(_proj_kernel, cfgs=tuple(cfgs), ropes=tuple(ropes),
                             group_sizes=group_sizes, tn=SEC)
    in_specs = [
        pl.BlockSpec((tm, D), lambda i, j: (i, 0)),
        pl.BlockSpec((1, D), lambda i, j: (0, 0)),
        pl.BlockSpec((D, SEC), lambda i, j: (0, j)),
        pl.BlockSpec((1, SEC), lambda i, j: (0, j)),
        pl.BlockSpec((1, SEC), lambda i, j: (0, j)),
    ]
    in_specs += [pl.BlockSpec((tm, LANES), lambda i, j: (i, 0)) for _ in tabs]
    in_specs += [pl.BlockSpec((SEC, SEC), lambda i, j: (0, 0)) for _ in bds]
    return pl.pallas_call(
        kern,
        grid=(M // tm, nsec),
        in_specs=in_specs,
        out_specs=pl.BlockSpec((tm, SEC), lambda i, j: (i, j)),
        out_shape=jax.ShapeDtypeStruct((M, N), F32),
        scratch_shapes=[pltpu.VMEM((tm, D), BF16)],
        compiler_params=_cparams("parallel", "arbitrary"),
        name="proj",
    )(x, gnorm.reshape(1, D), w, gain, rmask, *tabs, *bds)


def _matres_kernel(a_ref, w_ref, x_ref, o_ref):
    o_ref[...] = x_ref[...] + _dot(a_ref[...].astype(BF16), w_ref[...])


def _matres(a, w, x):
    M, K = a.shape
    N = w.shape[1]
    tm = _pick(M, 1024)
    return pl.pallas_call(
        _matres_kernel,
        grid=(M // tm,),
        in_specs=[
            pl.BlockSpec((tm, K), lambda i: (i, 0)),
            pl.BlockSpec((K, N), lambda i: (0, 0)),
            pl.BlockSpec((tm, N), lambda i: (i, 0)),
        ],
        out_specs=pl.BlockSpec((tm, N), lambda i: (i, 0)),
        out_shape=jax.ShapeDtypeStruct((M, N), F32),
        compiler_params=_cparams("parallel"),
        name="matres",
    )(a, w, x)


def _stack_masked(q, n_groups, width):
    t, L = q.shape
    lane_grp = _iota((t, L), 1) // width
    return jnp.concatenate([jnp.where(lane_grp == r, q, 0.0) for r in range(n_groups)], axis=0)


def _log_sigmoid(z):
    return jnp.minimum(z, 0.0) - jnp.log(1.0 + jnp.exp(-jnp.abs(z)))


def _topk_select(score, valid, ksel):
    R, L = score.shape
    score = jnp.where(valid, score, -jnp.inf)
    bits = lax.bitcast_convert_type(score, I32)
    key = jnp.where(bits < 0, bits ^ jnp.int32(0x7FFFFFFF), bits)
    kf = jnp.float32(ksel)

    def count_ge(t):
        return jnp.sum((key >= t).astype(F32), axis=1, keepdims=True)

    int_min = jnp.int32(-2 ** 31)
    thr = jnp.where(count_ge(jnp.zeros((R, 1), I32)) >= kf, jnp.int32(0), int_min)

    def body(it, thr):
        cand = thr | lax.shift_left(jnp.int32(1), 30 - it)
        return jnp.where(count_ge(cand) >= kf, cand, thr)

    thr = lax.fori_loop(0, 31, body, thr)
    need = kf - jnp.sum((key > thr).astype(F32), axis=1, keepdims=True)
    validf = jnp.where(valid, 1.0, 0.0)
    incl = (_iota((LANES, LANES), 0) <= _iota((LANES, LANES), 1)).astype(BF16)
    run = jnp.zeros((R, 1), F32)
    outs = []
    for c in range(L // LANES):
        sl = slice(c * LANES, (c + 1) * LANES)
        key_c = key[:, sl]
        eqf = jnp.where(key_c == thr, 1.0, 0.0)
        rank = _dot(eqf.astype(BF16), incl) + run
        run = run + jnp.sum(eqf, axis=1, keepdims=True)
        take = jnp.where(key_c > thr, 1.0, jnp.where(rank <= need, eqf, 0.0))
        outs.append(take * validf[:, sl])
    return jnp.concatenate(outs, axis=1)


def _top_blocks(gate, ok, topb):
    R, NB = gate.shape
    n = _iota((R, NB), 1)
    g = jnp.where(ok, gate, -jnp.inf)
    sel = jnp.zeros((R, NB), F32)
    for _ in range(topb):
        mx = jnp.max(g, axis=1, keepdims=True)
        idx = jnp.min(jnp.where(g == mx, n, NB), axis=1, keepdims=True)
        pick = n == jnp.where(mx > -jnp.inf, idx, -1)
        sel = jnp.where(pick, 1.0, sel)
        g = jnp.where(pick, -jnp.inf, g)
    return sel


def _pair_out(a, tq):
    lane = _iota((tq, LANES), 1)
    return jnp.where(lane < HEAD_DIM, a[:tq], a[tq:])


def _by_extent(i, tq, T, wb, fn):
    need = ((i + 1) * tq + wb - 1) // wb
    for v in range(1, T // wb + 1):
        pl.when(need == v)(functools.partial(fn, v * wb))


def _extent_step(T):
    return _pick(T, max(T // 4, 256))


def _row_pos(i, tq, rows):
    return i * tq + _iota((rows, 1), 0) % tq


def _sb_prompt_kernel(q_ref, k_ref, v_ref, o_ref, *, tq, T, wb, ck):
    i = pl.program_id(2)
    R = 2 * tq
    q2 = _stack_masked(q_ref[...], 2, HEAD_DIM).astype(BF16)
    upper = (_iota((ck, ck), 0) > _iota((ck, ck), 1)).astype(BF16)
    qpos = _row_pos(i, tq, R)
    col = _iota((R, ck), 1)

    def run(W):
        nc = W // ck
        ls, ms, masks = [], [], []
        for c in range(nc):
            z = _dot_nt(q2, k_ref[c * ck:(c + 1) * ck, :].astype(BF16)) * (HEAD_DIM ** -0.5)
            mask = (c * ck + col) < qpos
            l_c = _log_sigmoid(z)
            ls.append(l_c)
            ms.append(jnp.where(mask, l_c - z, 0.0))
            masks.append(mask)
        carry = jnp.zeros((R, 1), F32)
        acc = jnp.zeros((R, LANES), F32)
        for c in range(nc - 1, -1, -1):
            suffix = _dot_hilo(ms[c], upper) + carry
            carry = carry + jnp.sum(ms[c], axis=1, keepdims=True)
            w = jnp.where(masks[c], jnp.exp(ls[c] + suffix), 0.0)
            acc = acc + _dot(w.astype(BF16), v_ref[c * ck:(c + 1) * ck, :].astype(BF16))
        o_ref[...] = _pair_out(acc, tq)

    _by_extent(i, tq, T, wb, run)


def _sb_prompt(Y, B, T, qc, kc, vc):
    tq = _pick(T, 128)
    nq = T // tq
    cpb = SEC // LANES
    wb = _extent_step(T)
    return pl.pallas_call(
        functools.partial(_sb_prompt_kernel, tq=tq, T=T, wb=wb, ck=_pick(wb, 256)),
        grid=(B, cpb, nq),
        in_specs=[
            pl.BlockSpec((tq, LANES), lambda b, p, i: (b * nq + i, qc * cpb + p)),
            pl.BlockSpec((T, LANES), lambda b, p, i: (b, kc * cpb + p)),
            pl.BlockSpec((T, LANES), lambda b, p, i: (b, vc * cpb + p)),
        ],
        out_specs=pl.BlockSpec((tq, LANES), lambda b, p, i: (b * nq + i, p)),
        out_shape=jax.ShapeDtypeStruct((B * T, SEC), F32),
        compiler_params=_cparams("parallel", "parallel", "arbitrary"),
        name="sb_prompt",
    )(Y, Y, Y)


def _idx_scores(iq, ki_bf16, transposed=False):
    score = None
    w_off = IDX_HEADS * IDX_DIM + IDX_DIM
    for h in range(IDX_HEADS):
        qh = iq[:, h * IDX_DIM:(h + 1) * IDX_DIM].astype(BF16)
        s = _dot(qh, ki_bf16) if transposed else _dot_nt(qh, ki_bf16)
        term = jnp.maximum(s, 0.0) * iq[:, w_off + h:w_off + h + 1]
        score = term if score is None else score + term
    return score * (IDX_DIM ** -0.5)


def _dsa_prompt_kernel(q_ref, iq_ref, k_ref, v_ref, ik_ref, o_ref, *, tq, T, wb, ksel):
    i = pl.program_id(1)
    ki_off = IDX_HEADS * IDX_DIM

    def run(W):
        ki = ik_ref[0:W, ki_off:ki_off + IDX_DIM].astype(BF16)
        score = _idx_scores(iq_ref[...], ki)
        causal = _iota((tq, W), 1) <= _row_pos(i, tq, tq)
        sel = _topk_select(score, causal, ksel)
        sel2 = jnp.concatenate([sel, sel], axis=0)
        for p in range(SEC // LANES):
            sl = slice(p * LANES, (p + 1) * LANES)
            q2 = _stack_masked(q_ref[:, sl], 2, HEAD_DIM).astype(BF16)
            s_chunks = []
            for c in range(W // wb):
                ks = slice(c * wb, (c + 1) * wb)
                s = _dot_nt(q2, k_ref[ks, sl].astype(BF16)) * (HEAD_DIM ** -0.5)
                s_chunks.append(jnp.where(sel2[:, ks] > 0.0, s, NEG))
            o2 = _softmax_av(
                s_chunks, lambda pe, c: _dot(pe.astype(BF16), v_ref[c * wb:(c + 1) * wb, sl].astype(BF16)))
            o_ref[:, sl] = _pair_out(o2, tq)

    _by_extent(i, tq, T, wb, run)


def _dsa_prompt(Y, B, T, qc, kc, vc, ic):
    tq = _pick(T, 128)
    nq = T // tq
    ksel = min(DSA_TOPK, T // 4)
    return pl.pallas_call(
        functools.partial(_dsa_prompt_kernel, tq=tq, T=T, wb=_extent_step(T), ksel=ksel),
        grid=(B, nq),
        in_specs=[
            pl.BlockSpec((tq, SEC), lambda b, i: (b * nq + i, qc)),
            pl.BlockSpec((tq, SEC), lambda b, i: (b * nq + i, ic)),
            pl.BlockSpec((T, SEC), lambda b, i: (b, kc)),
            pl.BlockSpec((T, SEC), lambda b, i: (b, vc)),
            pl.BlockSpec((T, SEC), lambda b, i: (b, ic)),
        ],
        out_specs=pl.BlockSpec((tq, SEC), lambda b, i: (b * nq + i, 0)),
        out_shape=jax.ShapeDtypeStruct((B * T, SEC), F32),
        compiler_params=_cparams("parallel", "arbitrary"),
        name="dsa_prompt",
    )(Y, Y, Y, Y, Y)


def _softmax_av(s_chunks, pv):
    mx = jnp.max(functools.reduce(jnp.maximum, s_chunks), axis=1, keepdims=True)
    lsum, acc = None, None
    for c, s in enumerate(s_chunks):
        p = jnp.exp(s - mx)
        a = pv(p, c)
        lsum = p if lsum is None else lsum + p
        acc = a if acc is None else acc + a
    return acc / jnp.sum(lsum, axis=1, keepdims=True)


def _diff_prompt_kernel(lam_ref, q_ref, k_ref, v_ref, gs_ref, o_ref, *, tq, T, wb, ck, out_scale):
    i = pl.program_id(2)
    R = 4 * tq
    q4 = _stack_masked(q_ref[...], 4, DIFF_DIM).astype(BF16)
    qpos = _row_pos(i, tq, R)
    col = _iota((R, ck), 1)

    def run(W):
        s_chunks = []
        for c in range(W // ck):
            s = _dot_nt(q4, k_ref[c * ck:(c + 1) * ck, :].astype(BF16)) * (DIFF_DIM ** -0.5)
            s_chunks.append(jnp.where((c * ck + col) <= qpos, s, NEG))
        a = _softmax_av(
            s_chunks, lambda p, c: _dot(p.astype(BF16), v_ref[c * ck:(c + 1) * ck, :].astype(BF16)))
        finish(a)

    def finish(a):
        lam = lam_ref[...]
        lane = _iota((tq, LANES), 1)
        low = lane < HEAD_DIM
        o = jnp.where(low, a[:tq] - lam * a[tq:2 * tq], a[2 * tq:3 * tq] - lam * a[3 * tq:])
        o2 = o * o
        s0 = jnp.sum(jnp.where(low, o2, 0.0), axis=1, keepdims=True)
        s1 = jnp.sum(jnp.where(low, 0.0, o2), axis=1, keepdims=True)
        ssq = jnp.where(low, s0, s1)
        o_ref[...] = o * lax.rsqrt(ssq * (1.0 / HEAD_DIM) + EPS) * gs_ref[...] * out_scale

    _by_extent(i, tq, T, wb, run)


def _diff_prompt(Y, B, T, qc, kc, vc, lam, gsub, lam_init):
    tq = _pick(T, 128)
    nq = T // tq
    cpb = SEC // LANES
    wb = _extent_step(T)
    return pl.pallas_call(
        functools.partial(_diff_prompt_kernel, tq=tq, T=T, wb=wb, ck=_pick(wb, 256),
                          out_scale=1.0 - lam_init),
        grid=(B, cpb, nq),
        in_specs=[
            pl.BlockSpec((1, 1), lambda b, p, i: (0, 0)),
            pl.BlockSpec((tq, LANES), lambda b, p, i: (b * nq + i, qc * cpb + p)),
            pl.BlockSpec((T, LANES), lambda b, p, i: (b, kc * cpb + p)),
            pl.BlockSpec((T, LANES), lambda b, p, i: (b, vc * cpb + p)),
            pl.BlockSpec((1, LANES), lambda b, p, i: (0, 0)),
        ],
        out_specs=pl.BlockSpec((tq, LANES), lambda b, p, i: (b * nq + i, p)),
        out_shape=jax.ShapeDtypeStruct((B * T, SEC), F32),
        compiler_params=_cparams("parallel", "parallel", "arbitrary"),
        name="diff_prompt",
    )(lam, Y, Y, Y, jnp.tile(gsub, 2).reshape(1, LANES))


def _kmean_kernel(k_ref, o_ref):
    n = pl.program_id(1)

    @pl.when(n == 0)
    def _():
        o_ref[...] = jnp.zeros_like(o_ref)

    o_ref[0, pl.ds(n, 1), :] = jnp.sum(k_ref[...], axis=0, keepdims=True) * (1.0 / MOBA_BLOCK)


def _kmean(Y, B, T, kc, nbp):
    n_blk = T // MOBA_BLOCK
    return pl.pallas_call(
        _kmean_kernel,
        grid=(B, n_blk),
        in_specs=[pl.BlockSpec((MOBA_BLOCK, SEC), lambda b, n: (b * n_blk + n, kc))],
        out_specs=pl.BlockSpec((1, nbp, SEC), lambda b, n: (b, 0, 0)),
        out_shape=jax.ShapeDtypeStruct((B, nbp, SEC), F32),
        compiler_params=_cparams("parallel", "arbitrary"),
        name="kmean",
    )(Y)


def _moba_prompt_kernel(q_ref, km_ref, k_ref, v_ref, o_ref, *, tq, T, wb, topb):
    i = pl.program_id(2)
    R = 2 * tq
    ck = MOBA_BLOCK
    q2f = _stack_masked(q_ref[...], 2, HEAD_DIM)
    q2 = q2f.astype(BF16)
    gate = _dot_nt_hilo2(q2f, km_ref[0])
    n = _iota(gate.shape, 1)
    own = (i * tq) // MOBA_BLOCK
    allowed = jnp.where(n == own, 1.0, _top_blocks(gate, n < own, topb))
    qpos = _row_pos(i, tq, R)
    col = _iota((R, ck), 1)

    def run(W):
        s_chunks = []
        for c in range(W // ck):
            s = _dot_nt(q2, k_ref[c * ck:(c + 1) * ck, :].astype(BF16)) * (HEAD_DIM ** -0.5)
            ok = jnp.where((c * ck + col) <= qpos, allowed[:, c:c + 1], 0.0)
            s_chunks.append(jnp.where(ok > 0.0, s, NEG))
        a = _softmax_av(
            s_chunks, lambda p, c: _dot(p.astype(BF16), v_ref[c * ck:(c + 1) * ck, :].astype(BF16)))
        o_ref[...] = _pair_out(a, tq)

    _by_extent(i, tq, T, wb, run)


def _moba_prompt(Y, B, T, qc, kc, vc):
    assert T % MOBA_BLOCK == 0
    n_blk = T // MOBA_BLOCK
    nbp = -(-n_blk // LANES) * LANES
    topb = max(1, min(MOBA_TOPK, n_blk - 1))
    km = _kmean(Y, B, T, kc, nbp)
    tq = _pick(MOBA_BLOCK, 128)
    nq = T // tq
    cpb = SEC // LANES
    wb = _extent_step(T)
    assert wb % MOBA_BLOCK == 0
    return pl.pallas_call(
        functools.partial(_moba_prompt_kernel, tq=tq, T=T, wb=wb, topb=topb),
        grid=(B, cpb, nq),
        in_specs=[
            pl.BlockSpec((tq, LANES), lambda b, p, i: (b * nq + i, qc * cpb + p)),
            pl.BlockSpec((1, nbp, LANES), lambda b, p, i: (b, 0, p)),
            pl.BlockSpec((T, LANES), lambda b, p, i: (b, kc * cpb + p)),
            pl.BlockSpec((T, LANES), lambda b, p, i: (b, vc * cpb + p)),
        ],
        out_specs=pl.BlockSpec((tq, LANES), lambda b, p, i: (b * nq + i, p)),
        out_shape=jax.ShapeDtypeStruct((B * T, SEC), F32),
        compiler_params=_cparams("parallel", "parallel", "arbitrary"),
        name="moba_prompt",
    )(Y, km, Y, Y)


def _cross_kernel(q_ref, mk_ref, mv_ref, o_ref):
    for h in range(MEM_HEADS):
        sl = slice(h * MEM_HEAD_DIM, (h + 1) * MEM_HEAD_DIM)
        s = _dot_nt(q_ref[:, sl].astype(BF16), mk_ref[:, sl].astype(BF16)) * (MEM_HEAD_DIM ** -0.5)
        mx = jnp.max(s, axis=1, keepdims=True)
        p = jnp.exp(s - mx)
        l = jnp.sum(p, axis=1, keepdims=True)
        o_ref[:, sl] = _dot(p.astype(BF16), mv_ref[:, sl].astype(BF16)) / l


def _cross_rows_kernel(q_ref, mk_ref, mv_ref, o_ref):
    mk = mk_ref[...].astype(BF16)
    mv = mv_ref[...].astype(BF16)
    head_of_col = _iota((q_ref.shape[0], mk.shape[0]), 1) % MEM_HEADS
    for h in range(MEM_HEADS):
        sl = slice(h * MEM_HEAD_DIM, (h + 1) * MEM_HEAD_DIM)
        s = _dot_nt(q_ref[:, sl].astype(BF16), mk) * (MEM_HEAD_DIM ** -0.5)
        s = jnp.where(head_of_col == h, s, NEG)
        p = jnp.exp(s - jnp.max(s, axis=1, keepdims=True))
        o_ref[:, sl] = _dot(p.astype(BF16), mv) / jnp.sum(p, axis=1, keepdims=True)


def _cross_rows(q, row0, B, Tq, mk, mv, rows, blk_of):
    r0 = row0 // Tq
    assert row0 % Tq == 0
    return pl.pallas_call(
        _cross_rows_kernel,
        grid=(B,),
        in_specs=[
            pl.BlockSpec((Tq, SEC), lambda b: (r0 + b, 0)),
            pl.BlockSpec((rows, MEM_HEAD_DIM), lambda b: (blk_of(b), 0)),
            pl.BlockSpec((rows, MEM_HEAD_DIM), lambda b: (blk_of(b), 0)),
        ],
        out_specs=pl.BlockSpec((Tq, SEC), lambda b: (b, 0)),
        out_shape=jax.ShapeDtypeStruct((B * Tq, SEC), F32),
        compiler_params=_cparams("parallel"),
        name="cross_rows",
    )(q, mk, mv)


def _cross(q, row0, B, Tq, mk, mv, mt, k_map, v_map):
    tq = _pick(Tq, 256)
    nq = Tq // tq
    r0 = row0 // tq
    assert row0 % tq == 0
    return pl.pallas_call(
        _cross_kernel,
        grid=(B, nq),
        in_specs=[
            pl.BlockSpec((tq, SEC), lambda b, i: (r0 + b * nq + i, 0)),
            pl.BlockSpec((mt, SEC), k_map),
            pl.BlockSpec((mt, SEC), v_map),
        ],
        out_specs=pl.BlockSpec((tq, SEC), lambda b, i: (b * nq + i, 0)),
        out_shape=jax.ShapeDtypeStruct((B * Tq, SEC), F32),
        compiler_params=_cparams("parallel", "arbitrary"),
        name="cross",
    )(q, mk, mv)


def _pad_rows(x, rows):
    return jnp.concatenate([x, jnp.zeros((rows - x.shape[0], x.shape[1]), x.dtype)], axis=0)


def _dec_positions(R, Ts, page, n_pages, past_len):
    col = _iota((R, page), 1)
    qpos = past_len + _iota((R, page), 0) % Ts
    kpos = [c * page + col for c in range(n_pages)] + [past_len + col]
    exists = [None] * n_pages + [col < Ts]
    return qpos, kpos, exists


def _diag_out(acc, n_heads, Ts, step=1):
    head = _iota((Ts, SEC), 1) // HEAD_DIM
    out = jnp.zeros((Ts, SEC), F32)
    for h in range(n_heads):
        r = step * h * Ts
        out = jnp.where(head == h, acc[r:r + Ts], out)
    return out


class _PagedKV:
    def __init__(self, kn_ref, vn_ref, kp, vp, page):
        self.kp, self.vp, self.page, self.n = kp, vp, page, len(kp)
        self.kn = _pad_rows(kn_ref[...], page).astype(BF16)
        self.vn = _pad_rows(vn_ref[...], page).astype(BF16)

    def kt(self, c):
        return self.kp[c][...].reshape(SEC, self.page)

    def scores(self, q_bf16, c):
        if c < self.n:
            return _dot(q_bf16, self.kt(c).astype(BF16))
        return _dot_nt(q_bf16, self.kn)

    def pv(self, p, c):
        if c < self.n:
            return _dot_nt(p.astype(BF16), self.vp[c][...].reshape(SEC, self.page).astype(BF16))
        return _dot(p.astype(BF16), self.vn)


def _softmax_chunks(logits, masks, kv):
    return _softmax_av([jnp.where(m, s, NEG) for s, m in zip(logits, masks)], kv.pv)


def _sb_dec_kernel(pt_ref, q_ref, kn_ref, vn_ref, *rest, n_pages, Ts, page, past_len):
    kp, vp, o_ref = rest[:n_pages], rest[n_pages:2 * n_pages], rest[2 * n_pages]
    H = SEC // HEAD_DIM
    R = H * Ts
    qbd = _stack_masked(q_ref[...], H, HEAD_DIM).astype(BF16)
    kv = _PagedKV(kn_ref, vn_ref, kp, vp, page)
    qpos, kpos, exists = _dec_positions(R, Ts, page, n_pages, past_len)
    upper = (_iota((page, page), 0) > _iota((page, page), 1)).astype(BF16)
    ls, ms, masks = [], [], []
    for c in range(n_pages + 1):
        z = kv.scores(qbd, c) * (HEAD_DIM ** -0.5)
        mask = kpos[c] < qpos
        if exists[c] is not None:
            mask = mask & exists[c]
        l_c = _log_sigmoid(z)
        ls.append(l_c)
        ms.append(jnp.where(mask, l_c - z, 0.0))
        masks.append(mask)
    run = jnp.zeros((R, 1), F32)
    acc = jnp.zeros((R, SEC), F32)
    for c in range(n_pages, -1, -1):
        suffix = _dot_hilo(ms[c], upper) + run
        run = run + jnp.sum(ms[c], axis=1, keepdims=True)
        w = jnp.where(masks[c], jnp.exp(ls[c] + suffix), 0.0)
        acc = acc + kv.pv(w, c)
    o_ref[...] = _diag_out(acc, H, Ts)


def _dsa_dec_kernel(pt_ref, q_ref, iq_ref, kn_ref, vn_ref, *rest, n_pages, Ts, page, past_len, ksel):
    kp, vp = rest[:n_pages], rest[n_pages:2 * n_pages]
    ip, o_ref = rest[2 * n_pages:3 * n_pages], rest[3 * n_pages]
    H = SEC // HEAD_DIM
    iq = iq_ref[...]
    ki_off = IDX_HEADS * IDX_DIM
    chunks = [_idx_scores(iq, r[...].astype(BF16), transposed=True) for r in ip]
    chunks.append(_idx_scores(iq, _pad_rows(iq[:, ki_off:ki_off + IDX_DIM], page).astype(BF16)))
    score = jnp.concatenate(chunks, axis=1)
    kcol = _iota(score.shape, 1)
    valid = (kcol <= past_len + _iota(score.shape, 0)) & (kcol < past_len + Ts)
    sel = _topk_select(score, valid, ksel)
    qbd = _stack_masked(q_ref[...], H, HEAD_DIM).astype(BF16)
    kv = _PagedKV(kn_ref, vn_ref, kp, vp, page)
    logits, masks = [], []
    for c in range(n_pages + 1):
        logits.append(kv.scores(qbd, c) * (HEAD_DIM ** -0.5))
        sel_c = sel[:, c * page:(c + 1) * page]
        masks.append(jnp.concatenate([sel_c] * H, axis=0) > 0.0)
    o_ref[...] = _diag_out(_softmax_chunks(logits, masks, kv), H, Ts)


def _diff_dec_kernel(pt_ref, lam_ref, q_ref, kn_ref, vn_ref, gs_ref, bd_ref, *rest,
                     n_pages, Ts, page, past_len, out_scale):
    kp, vp, o_ref = rest[:n_pages], rest[n_pages:2 * n_pages], rest[2 * n_pages]
    G = SEC // DIFF_DIM
    R = G * Ts
    qbd = _stack_masked(q_ref[...], G, DIFF_DIM).astype(BF16)
    kv = _PagedKV(kn_ref, vn_ref, kp, vp, page)
    qpos, kpos, exists = _dec_positions(R, Ts, page, n_pages, past_len)
    logits, masks = [], []
    for c in range(n_pages + 1):
        logits.append(kv.scores(qbd, c) * (DIFF_DIM ** -0.5))
        mask = kpos[c] <= qpos
        masks.append(mask if exists[c] is None else mask & exists[c])
    a = _softmax_chunks(logits, masks, kv)
    lam = lam_ref[...]
    d = jnp.concatenate([a[(2 * h) * Ts:(2 * h + 1) * Ts] - lam * a[(2 * h + 1) * Ts:(2 * h + 2) * Ts]
                         for h in range(G // 2)], axis=0)
    o = _diag_out(d, G // 2, Ts)
    ssq = _dot_hilo(o * o, bd_ref[...])
    o_ref[...] = o * lax.rsqrt(ssq * (1.0 / HEAD_DIM) + EPS) * gs_ref[...] * out_scale


def _moba_dec_kernel(pt_ref, q_ref, kn_ref, vn_ref, *rest, n_pages, Ts, page, past_len, topb):
    kp, vp, o_ref = rest[:n_pages], rest[n_pages:2 * n_pages], rest[2 * n_pages]
    H = SEC // HEAD_DIM
    R = H * Ts
    ppb = MOBA_BLOCK // page
    own = past_len // MOBA_BLOCK
    qbdf = _stack_masked(q_ref[...], H, HEAD_DIM)
    qbd = qbdf.astype(BF16)
    kv = _PagedKV(kn_ref, vn_ref, kp, vp, page)
    nb_lane = _iota((SEC, LANES), 1)
    kmean_t = jnp.zeros((SEC, LANES), F32)
    for n in range(own):
        blk = functools.reduce(jnp.add, [kv.kt(c) for c in range(n * ppb, (n + 1) * ppb)])
        kmean_t = jnp.where(nb_lane == n, jnp.sum(blk, axis=1, keepdims=True) * (1.0 / MOBA_BLOCK), kmean_t)
    gate = _dot_hilo2(qbdf, kmean_t)
    sel = _top_blocks(gate, _iota((R, LANES), 1) < own, topb)
    qpos, kpos, exists = _dec_positions(R, Ts, page, n_pages, past_len)
    logits, masks = [], []
    for c in range(n_pages + 1):
        logits.append(kv.scores(qbd, c) * (HEAD_DIM ** -0.5))
        if c < n_pages:
            b = c // ppb
            masks.append(jnp.broadcast_to(sel[:, b:b + 1], (R, page)) > 0.0)
        else:
            masks.append((kpos[c] <= qpos) & exists[c])
    o_ref[...] = _diag_out(_softmax_chunks(logits, masks, kv), H, Ts)


def _dec_call(kern, name, page_table, Y, row0, Ts, cache_k, cache_v, layer, half, q_cols, extra=(),
              idx_cache=None, idx_layer=0):
    Bs, n_pages = page_table.shape
    page = cache_k.shape[4]
    hpb = SEC // HEAD_DIM
    rb = row0 // Ts

    def ysec(col):
        return pl.BlockSpec((Ts, SEC), lambda b, pt, col=col: (rb + b, col))

    def pspec(p):
        return pl.BlockSpec((None, None, hpb, HEAD_DIM, page),
                            lambda b, pt, p=p: (pt[b, p], layer, half, 0, 0))

    def ispec(p):
        return pl.BlockSpec((None, None, IDX_DIM, page), lambda b, pt, p=p: (pt[b, p], idx_layer, 0, 0))

    in_specs = []
    ops = []
    lead = [e for e in extra if e[0] == "lead"]
    tail = [e for e in extra if e[0] == "tail"]
    for _, arr, spec in lead:
        ops.append(arr)
        in_specs.append(spec)
    for col in q_cols:
        ops.append(Y)
        in_specs.append(ysec(col))
    for _, arr, spec in tail:
        ops.append(arr)
        in_specs.append(spec)
    for p in range(n_pages):
        ops.append(cache_k)
        in_specs.append(pspec(p))
    for p in range(n_pages):
        ops.append(cache_v)
        in_specs.append(pspec(p))
    if idx_cache is not None:
        for p in range(n_pages):
            ops.append(idx_cache)
            in_specs.append(ispec(p))
    return pl.pallas_call(
        kern,
        grid_spec=pltpu.PrefetchScalarGridSpec(
            num_scalar_prefetch=1,
            grid=(Bs,),
            in_specs=in_specs,
            out_specs=pl.BlockSpec((Ts, SEC), lambda b, pt: (b, 0)),
        ),
        out_shape=jax.ShapeDtypeStruct((Bs * Ts, SEC), F32),
        compiler_params=_cparams("arbitrary"),
        name=name,
    )(page_table, *ops)


def kernel(x_prompt, x_sample, mem_prompt, cache_k, cache_v, cache_idx_k, cache_mem_k, cache_mem_v, page_table, g_norm, w_ffn1_gu, w_ffn1_d, w_ffn2_gu, w_ffn2_d, w_in_even, g_qk_b, w_in_odd, g_qk_c, g_qk_d, g_sub_c, lambda_c, w_out, w_mem_q, w_mem_kv, w_mem_o, g_mem_qk):
    Bp, Tp, D = x_prompt.shape
    Bs, Ts, _ = x_sample.shape
    depth = g_norm.shape[0]
    n_phys, _, page, n_kv, hd = cache_k.shape
    n_pages = page_table.shape[1]
    past_len = n_pages * page
    mt = mem_prompt.shape[1]
    Mp, Ms = Bp * Tp, Bs * Ts
    assert D == 2 * SEC and n_kv * hd == D and hd == HEAD_DIM
    assert Mp % Ts == 0 and past_len % MOBA_BLOCK == 0 and MOBA_BLOCK % page == 0 and Ts <= page

    x = jnp.concatenate([x_prompt.reshape(Mp, D), x_sample.reshape(Ms, D)], axis=0)
    mem = mem_prompt.reshape(Bp * mt, D)
    pos = jnp.concatenate([jnp.tile(jnp.arange(Tp, dtype=I32), Bp),
                           jnp.tile(jnp.arange(Ts, dtype=I32) + past_len, Bs)])
    tab_a = _rope_tables(pos, HEAD_DIM, ROT_DIM)
    tab_b = _rope_tables(pos, DIFF_DIM, DIFF_ROT)
    ck = jnp.transpose(cache_k, (0, 1, 3, 4, 2))
    cv = jnp.transpose(cache_v, (0, 1, 3, 4, 2))
    cik = jnp.transpose(cache_idx_k, (0, 1, 3, 2))
    cmk = cache_mem_k.reshape(Bs * depth * mt * MEM_HEADS, MEM_HEAD_DIM)
    cmv = cache_mem_v.reshape(Bs * depth * mt * MEM_HEADS, MEM_HEAD_DIM)
    ones = jnp.ones((SEC,), F32)
    zeros = jnp.zeros((SEC,), F32)
    idx_w = IDX_HEADS * IDX_DIM + IDX_DIM + IDX_HEADS
    idx_on = (jnp.arange(SEC) < IDX_HEADS * IDX_DIM + IDX_DIM).astype(F32)
    ksel_s = min(DSA_TOPK, (past_len + Ts) // 4)
    topb_s = max(1, min(MOBA_TOPK, -(-(past_len + Ts) // MOBA_BLOCK) - 1))
    bd64 = _group_ones(HEAD_DIM)

    Ys, mkvs = [], []
    for l in range(depth):
        g = g_norm[l]
        x = _ffn(x, g[0], w_ffn1_gu[l].astype(BF16), w_ffn1_d[l].astype(BF16))
        l2 = l // 2
        if l % 2 == 0:
            w = w_in_even[l2]
            wcols = [w[:, 0:512], w[:, 1536:2048], w[:, 512:1024], w[:, 2048:2560], w[:, 1024:1536],
                     w[:, 2560:3072], jnp.pad(w[:, 3072:3072 + idx_w], ((0, 0), (0, SEC - idx_w)))]
            cfgs = [(0, -1), (HEAD_DIM, 0), (0, -1), (HEAD_DIM, 0), (0, -1), (0, -1), (0, 0)]
            gains = [ones, jnp.tile(g_qk_b[l2, 0], 8), ones, jnp.tile(g_qk_b[l2, 1], 8), ones, ones, ones]
            rmask = [zeros, ones, zeros, ones, zeros, zeros, idx_on]
            tabs, ropes = list(tab_a), [(HEAD_DIM, ROT_DIM // 2)]
        else:
            w = w_in_odd[l2]
            wcols = [w[:, 0:512], w[:, 1536:2048], w[:, 512:1024], w[:, 2048:2560], w[:, 1024:1536],
                     w[:, 2560:3072]]
            cfgs = [(DIFF_DIM, 1), (HEAD_DIM, 0), (DIFF_DIM, 1), (HEAD_DIM, 0), (0, -1), (0, -1)]
            gains = [jnp.tile(g_qk_c[l2, 0], 8), jnp.tile(g_qk_d[l2, 0], 8), jnp.tile(g_qk_c[l2, 1], 8),
                     jnp.tile(g_qk_d[l2, 1], 8), ones, ones]
            rmask = [ones, ones, ones, ones, zeros, zeros]
            tabs = list(tab_a) + list(tab_b)
            ropes = [(HEAD_DIM, ROT_DIM // 2), (DIFF_DIM, DIFF_ROT // 2)]
        Y = _proj(x, g[1], jnp.concatenate(wcols, axis=1).astype(BF16),
                  jnp.concatenate(gains).reshape(1, -1), jnp.concatenate(rmask).reshape(1, -1),
                  cfgs, tabs, ropes)
        Ys.append(Y)

        if l % 2 == 0:
            oa_p = _sb_prompt(Y, Bp, Tp, 0, 2, 4)
            ob_p = _dsa_prompt(Y, Bp, Tp, 1, 3, 5, 6)
            oa_s = _dec_call(
                functools.partial(_sb_dec_kernel, n_pages=n_pages, Ts=Ts, page=page, past_len=past_len),
                "sb_dec", page_table, Y, Mp, Ts, ck, cv, l, 0, [0, 2, 4])
            ob_s = _dec_call(
                functools.partial(_dsa_dec_kernel, n_pages=n_pages, Ts=Ts, page=page, past_len=past_len,
                                  ksel=ksel_s),
                "dsa_dec", page_table, Y, Mp, Ts, ck, cv, l, 1, [1, 6, 3, 5],
                idx_cache=cik, idx_layer=l2)
        else:
            lam_init = 0.8 - 0.6 * math.exp(-0.3 * l)
            lp = lambda_c[l2].astype(F32)
            lam = (jnp.exp(jnp.sum(lp[0] * lp[1])) - jnp.exp(jnp.sum(lp[2] * lp[3])) + lam_init).reshape(1, 1)
            oa_p = _diff_prompt(Y, Bp, Tp, 0, 2, 4, lam, g_sub_c[l2], lam_init)
            ob_p = _moba_prompt(Y, Bp, Tp, 1, 3, 5)
            one = pl.BlockSpec((1, 1), lambda b, pt: (0, 0))
            oa_s = _dec_call(
                functools.partial(_diff_dec_kernel, n_pages=n_pages, Ts=Ts, page=page, past_len=past_len,
                                  out_scale=1.0 - lam_init),
                "diff_dec", page_table, Y, Mp, Ts, ck, cv, l, 0, [0, 2, 4],
                extra=[("lead", lam, one),
                       ("tail", jnp.tile(g_sub_c[l2], 8).reshape(1, SEC),
                        pl.BlockSpec((1, SEC), lambda b, pt: (0, 0))),
                       ("tail", bd64, pl.BlockSpec((SEC, SEC), lambda b, pt: (0, 0)))])
            ob_s = _dec_call(
                functools.partial(_moba_dec_kernel, n_pages=n_pages, Ts=Ts, page=page, past_len=past_len,
                                  topb=topb_s),
                "moba_dec", page_table, Y, Mp, Ts, ck, cv, l, 1, [1, 3, 5])
        o = jnp.concatenate([jnp.concatenate([oa_p, ob_p], axis=1),
                             jnp.concatenate([oa_s, ob_s], axis=1)], axis=0)
        x = _matres(o, w_out[l].astype(BF16), x)

        g_mq = jnp.tile(g_mem_qk[l, 0], MEM_HEADS).reshape(1, SEC)
        qm = _proj(x, g[2], w_mem_q[l].astype(BF16), g_mq, jnp.zeros((1, SEC), F32),
                   [(MEM_HEAD_DIM, -1)], [], [])
        g_mk = jnp.concatenate([jnp.tile(g_mem_qk[l, 1], MEM_HEADS), ones]).reshape(1, 2 * SEC)
        mkv = _proj(mem, g[4], w_mem_kv[l].astype(BF16), g_mk, jnp.zeros((1, 2 * SEC), F32),
                    [(MEM_HEAD_DIM, -1), (0, -1)], [], [])
        mkvs.append(mkv)
        oc_p = _cross(qm, 0, Bp, Tp, mkv, mkv, mt, lambda b, i: (b, 0), lambda b, i: (b, 1))
        oc_s = _cross_rows(qm, Mp, Bs, Ts, cmk, cmv, mt * MEM_HEADS, lambda b, l=l: b * depth + l)
        x = _matres(jnp.concatenate([oc_p, oc_s], axis=0), w_mem_o[l].astype(BF16), x)
        x = _ffn(x, g[3], w_ffn2_gu[l].astype(BF16), w_ffn2_d[l].astype(BF16))

    def stack(rows0, rows1, col0, col1, shape, arrs):
        return jnp.stack([a[rows0:rows1, col0:col1].reshape(shape) for a in arrs], axis=1)

    ki0 = 6 * SEC + IDX_HEADS * IDX_DIM
    y_prompt = x[:Mp].reshape(Bp, Tp, D)
    y_sample = x[Mp:].reshape(Bs, Ts, D)
    k_prompt = stack(0, Mp, 2 * SEC, 4 * SEC, (Bp, Tp, n_kv, hd), Ys)
    v_prompt = stack(0, Mp, 4 * SEC, 6 * SEC, (Bp, Tp, n_kv, hd), Ys)
    idx_k_prompt = stack(0, Mp, ki0, ki0 + IDX_DIM, (Bp, Tp, IDX_DIM), Ys[0::2])
    mem_k_prompt = stack(0, Bp * mt, 0, SEC, (Bp, mt, MEM_HEADS, MEM_HEAD_DIM), mkvs)
    mem_v_prompt = stack(0, Bp * mt, SEC, 2 * SEC, (Bp, mt, MEM_HEADS, MEM_HEAD_DIM), mkvs)
    k_sample = stack(Mp, Mp + Ms, 2 * SEC, 4 * SEC, (Bs, Ts, n_kv, hd), Ys)
    v_sample = stack(Mp, Mp + Ms, 4 * SEC, 6 * SEC, (Bs, Ts, n_kv, hd), Ys)
    idx_k_sample = stack(Mp, Mp + Ms, ki0, ki0 + IDX_DIM, (Bs, Ts, IDX_DIM), Ys[0::2])
    return (y_prompt, y_sample, k_prompt, v_prompt, idx_k_prompt, mem_k_prompt, mem_v_prompt,
            k_sample, v_sample, idx_k_sample)
```

```python
import functools
import math

import jax
import jax.numpy as jnp
from jax import lax
from jax.experimental import pallas as pl
from jax.experimental.pallas import tpu as pltpu

F32 = jnp.float32
BF16 = jnp.bfloat16
I32 = jnp.int32

EPS = 1e-6
HEAD_DIM = 64
DIFF_DIM = 32
ROT_DIM = 16
DIFF_ROT = 8
IDX_HEADS = 4
IDX_DIM = 64
DSA_TOPK = 256
MOBA_BLOCK = 256
MOBA_TOPK = 3
ROPE_THETA = 500000.0
MEM_HEADS = 4
MEM_HEAD_DIM = 128

LANES = 128
SEC = 512
NEG = -1e30
VMEM_LIMIT = 56 * 1024 * 1024


def _cparams(*sem):
    return pltpu.CompilerParams(dimension_semantics=sem, vmem_limit_bytes=VMEM_LIMIT)


def _pick(n, target):
    t = min(n, target)
    while t > 8 and (n % t or t % 8):
        t -= 1
    assert n % t == 0, (n, target)
    return t


def _dot(a, b):
    return jnp.dot(a, b, preferred_element_type=F32)


def _dot_nt(a, b):
    return lax.dot_general(a, b, (((1,), (1,)), ((), ())), preferred_element_type=F32)


def _split(x):
    hi = x.astype(BF16)
    lo = (x - hi.astype(F32)).astype(BF16)
    return hi, lo


def _dot_hilo(x, w_bf16):
    hi, lo = _split(x)
    return _dot(hi, w_bf16) + _dot(lo, w_bf16)


def _dot_nt_hilo2(a, b):
    ah, al = _split(a)
    bh, bl = _split(b)
    return _dot_nt(ah, bh) + _dot_nt(ah, bl) + _dot_nt(al, bh)


def _dot_hilo2(a, b):
    ah, al = _split(a)
    bh, bl = _split(b)
    return _dot(ah, bh) + _dot(ah, bl) + _dot(al, bh)


def _iota(shape, dim):
    return lax.broadcasted_iota(I32, shape, dim)


def _rms_rows(x, g):
    return x * lax.rsqrt(jnp.mean(x * x, axis=-1, keepdims=True) + EPS) * g


def _ffn_kernel(x_ref, g_ref, wg_ref, wu_ref, wd_ref, o_ref, h_scr, acc_scr):
    f = pl.program_id(1)

    @pl.when(f == 0)
    def _():
        h_scr[...] = _rms_rows(x_ref[...], g_ref[...]).astype(BF16)
        acc_scr[...] = jnp.zeros_like(acc_scr)

    h = h_scr[...]
    g = _dot(h, wg_ref[...])
    u = _dot(h, wu_ref[...])
    a = (g * (1.0 / (1.0 + jnp.exp(-g)))) * u
    acc_scr[...] += _dot(a.astype(BF16), wd_ref[...])

    @pl.when(f == pl.num_programs(1) - 1)
    def _():
        o_ref[...] = x_ref[...] + 0.5 * acc_scr[...]


def _ffn(x, g, w_gu, w_d):
    M, D = x.shape
    F = w_d.shape[0]
    tm = _pick(M, 1024)
    tf = _pick(F, 256)
    nf = F // tf
    return pl.pallas_call(
        _ffn_kernel,
        grid=(M // tm, nf),
        in_specs=[
            pl.BlockSpec((tm, D), lambda i, f: (i, 0)),
            pl.BlockSpec((1, D), lambda i, f: (0, 0)),
            pl.BlockSpec((D, tf), lambda i, f: (0, f)),
            pl.BlockSpec((D, tf), lambda i, f: (0, nf + f)),
            pl.BlockSpec((tf, D), lambda i, f: (f, 0)),
        ],
        out_specs=pl.BlockSpec((tm, D), lambda i, f: (i, 0)),
        out_shape=jax.ShapeDtypeStruct((M, D), F32),
        scratch_shapes=[pltpu.VMEM((tm, D), BF16), pltpu.VMEM((tm, D), F32)],
        compiler_params=_cparams("parallel", "arbitrary"),
        name="ffn",
    )(x, g.reshape(1, D), w_gu, w_gu, w_d)


def _group_ones(gs):
    r = jnp.arange(SEC) // gs
    return (r[:, None] == r[None, :]).astype(BF16)


def _rope_tables(pos, gs, rot):
    half = rot // 2
    freq = ROPE_THETA ** (-jnp.arange(half, dtype=F32) * 2.0 / rot)
    ang = pos.astype(F32)[:, None] * freq[None, :]
    cos, sin = jnp.cos(ang), jnp.sin(ang)
    d = jnp.arange(LANES) % gs
    c_full = cos[:, d % half]
    s_full = sin[:, d % half]
    C = jnp.where(d[None, :] < rot, c_full, 1.0)
    S = jnp.where(d[None, :] < half, -s_full, jnp.where(d[None, :] < rot, s_full, 0.0))
    return C, S


def _proj_kernel(*refs, cfgs, ropes, group_sizes, tn):
    x_ref, g_ref, w_ref, gain_ref, rmask_ref = refs[:5]
    n_tab = 2 * len(ropes)
    tab_refs = refs[5:5 + n_tab]
    bd_refs = refs[5 + n_tab:5 + n_tab + len(group_sizes)]
    o_ref, h_scr = refs[5 + n_tab + len(group_sizes):]
    j = pl.program_id(1)

    @pl.when(j == 0)
    def _():
        h_scr[...] = _rms_rows(x_ref[...], g_ref[...]).astype(BF16)

    y = _dot(h_scr[...], w_ref[...])
    tm = y.shape[0]

    def epilogue(y, cfg):
        gs, rope = cfg
        if gs:
            bd = bd_refs[group_sizes.index(gs)][...]
            ssq = _dot_hilo(y * y, bd)
            y = y * lax.rsqrt(ssq * (1.0 / gs) + EPS) * gain_ref[...]
        if rope < 0:
            o_ref[...] = y
            return
        period, half = ropes[rope]
        C = tab_refs[2 * rope][...]
        S = tab_refs[2 * rope + 1][...]
        first = (_iota((tm, LANES), 1) % period) < half
        for c in range(tn // LANES):
            sl = slice(c * LANES, (c + 1) * LANES)
            yc = y[:, sl]
            on = rmask_ref[:, sl] > 0.0
            partner = jnp.where(first, pltpu.roll(yc, LANES - half, 1), pltpu.roll(yc, half, 1))
            o_ref[:, sl] = yc * jnp.where(on, C, 1.0) + partner * jnp.where(on, S, 0.0)

    for cfg in sorted(set(cfgs)):
        cond = None
        for s, c in enumerate(cfgs):
            if c == cfg:
                cond = (j == s) if cond is None else (cond | (j == s))
        pl.when(cond)(functools.partial(epilogue, y, cfg))


def _proj(x, gnorm, w, gain, rmask, cfgs, tabs, ropes):
    M, D = x.shape
    N = w.shape[1]
    nsec = N // SEC
    assert nsec == len(cfgs)
    tm = _pick(M, 512)
    group_sizes = tuple(sorted({c[0] for c in cfgs if c[0]}))
    bds = [_group_ones(gs) for gs in group_sizes]
    kern = functools.partial(_proj_kernel, cfgs=tuple(cfgs), ropes=tuple(ropes),
                             group_sizes=group_sizes, tn=SEC)
    in_specs = [
        pl.BlockSpec((tm, D), lambda i, j: (i, 0)),
        pl.BlockSpec((1, D), lambda i, j: (0, 0)),
        pl.BlockSpec((D, SEC), lambda i, j: (0, j)),
        pl.BlockSpec((1, SEC), lambda i, j: (0, j)),
        pl.BlockSpec((1, SEC), lambda i, j: (0, j)),
    ]
    in_specs += [pl.BlockSpec((tm, LANES), lambda i, j: (i, 0)) for _ in tabs]
    in_specs += [pl.BlockSpec((SEC, SEC), lambda i, j: (0, 0)) for _ in bds]
    return pl.pallas_call(
        kern,
        grid=(M // tm, nsec),
        in_specs=in_specs,
        out_specs=pl.BlockSpec((tm, SEC), lambda i, j: (i, j)),
        out_shape=jax.ShapeDtypeStruct((M, N), F32),
        scratch_shapes=[pltpu.VMEM((tm, D), BF16)],
        compiler_params=_cparams("parallel", "arbitrary"),
        name="proj",
    )(x, gnorm.reshape(1, D), w, gain, rmask, *tabs, *bds)


def _matres_kernel(a_ref, w_ref, x_ref, o_ref):
    o_ref[...] = x_ref[...] + _dot(a_ref[...].astype(BF16), w_ref[...])


def _matres(a, w, x):
    M, K = a.shape
    N = w.shape[1]
    tm = _pick(M, 1024)
    return pl.pallas_call(
        _matres_kernel,
        grid=(M // tm,),
        in_specs=[
            pl.BlockSpec((tm, K), lambda i: (i, 0)),
            pl.BlockSpec((K, N), lambda i: (0, 0)),
            pl.BlockSpec((tm, N), lambda i: (i, 0)),
        ],
        out_specs=pl.BlockSpec((tm, N), lambda i: (i, 0)),
        out_shape=jax.ShapeDtypeStruct((M, N), F32),
        compiler_params=_cparams("parallel"),
        name="matres",
    )(a, w, x)


def _stack_masked(q, n_groups, width):
    t, L = q.shape
    lane_grp = _iota((t, L), 1) // width
    return jnp.concatenate([jnp.where(lane_grp == r, q, 0.0) for r in range(n_groups)], axis=0)


def _log_sigmoid(z):
    return jnp.minimum(z, 0.0) - jnp.log(1.0 + jnp.exp(-jnp.abs(z)))


def _topk_select(score, valid, ksel):
    R, L = score.shape
    score = jnp.where(valid, score, -jnp.inf)
    bits = lax.bitcast_convert_type(score, I32)
    key = jnp.where(bits < 0, bits ^ jnp.int32(0x7FFFFFFF), bits)
    kf = jnp.float32(ksel)

    def count_ge(t):
        return jnp.sum((key >= t).astype(F32), axis=1, keepdims=True)

    int_min = jnp.int32(-2 ** 31)
    thr = jnp.where(count_ge(jnp.zeros((R, 1), I32)) >= kf, jnp.int32(0), int_min)

    def body(it, thr):
        cand = thr | lax.shift_left(jnp.int32(1), 30 - it)
        return jnp.where(count_ge(cand) >= kf, cand, thr)

    thr = lax.fori_loop(0, 31, body, thr)
    need = kf - jnp.sum((key > thr).astype(F32), axis=1, keepdims=True)
    validf = jnp.where(valid, 1.0, 0.0)
    incl = (_iota((LANES, LANES), 0) <= _iota((LANES, LANES), 1)).astype(BF16)
    run = jnp.zeros((R, 1), F32)
    outs = []
    for c in range(L // LANES):
        sl = slice(c * LANES, (c + 1) * LANES)
        key_c = key[:, sl]
        eqf = jnp.where(key_c == thr, 1.0, 0.0)
        rank = _dot(eqf.astype(BF16), incl) + run
        run = run + jnp.sum(eqf, axis=1, keepdims=True)
        take = jnp.where(key_c > thr, 1.0, jnp.where(rank <= need, eqf, 0.0))
        outs.append(take * validf[:, sl])
    return jnp.concatenate(outs, axis=1)


def _top_blocks(gate, ok, topb):
    R, NB = gate.shape
    n = _iota((R, NB), 1)
    g = jnp.where(ok, gate, -jnp.inf)
    sel = jnp.zeros((R, NB), F32)
    for _ in range(topb):
        mx = jnp.max(g, axis=1, keepdims=True)
        idx = jnp.min(jnp.where(g == mx, n, NB), axis=1, keepdims=True)
        pick = n == jnp.where(mx > -jnp.inf, idx, -1)
        sel = jnp.where(pick, 1.0, sel)
        g = jnp.where(pick, -jnp.inf, g)
    return sel


def _pair_out(a, tq):
    lane = _iota((tq, LANES), 1)
    return jnp.where(lane < HEAD_DIM, a[:tq], a[tq:])


def _by_extent(i, tq, T, wb, fn):
    need = ((i + 1) * tq + wb - 1) // wb
    for v in range(1, T // wb + 1):
        pl.when(need == v)(functools.partial(fn, v * wb))


def _extent_step(T):
    return _pick(T, max(T // 8, 256))


PAIRS_PER_STEP = 2


def _row_pos(i, tq, rows):
    return i * tq + _iota((rows, 1), 0) % tq


def _pair_call(kern, name, Y, B, T, tq, qc, kc, vc, lead=(), mid=(), tail=()):
    nq = T // tq
    gw = PAIRS_PER_STEP * LANES
    ng = SEC // gw
    in_specs = [s for _, s in lead]
    in_specs.append(pl.BlockSpec((tq, gw), lambda b, p, i: (b * nq + i, qc * ng + p)))
    in_specs += [s for _, s in mid]
    in_specs.append(pl.BlockSpec((T, gw), lambda b, p, i: (b, kc * ng + p)))
    in_specs.append(pl.BlockSpec((T, gw), lambda b, p, i: (b, vc * ng + p)))
    in_specs += [s for _, s in tail]
    ops = [a for a, _ in lead] + [Y] + [a for a, _ in mid] + [Y, Y] + [a for a, _ in tail]
    return pl.pallas_call(
        kern,
        grid=(B, ng, nq),
        in_specs=in_specs,
        out_specs=pl.BlockSpec((tq, gw), lambda b, p, i: (b * nq + i, p)),
        out_shape=jax.ShapeDtypeStruct((B * T, SEC), F32),
        compiler_params=_cparams("parallel", "parallel", "arbitrary"),
        name=name,
    )(*ops)


def _sb_prompt_kernel(q_ref, k_ref, v_ref, o_ref, *, tq, T, wb, ck):
    i = pl.program_id(2)
    R = 2 * tq
    upper = (_iota((ck, ck), 0) > _iota((ck, ck), 1)).astype(BF16)
    qpos = _row_pos(i, tq, R)
    col = _iota((R, ck), 1)

    def run(W):
        nc = W // ck
        n_full = (W - wb) // ck
        outs = []
        for pp in range(PAIRS_PER_STEP):
            sl = slice(pp * LANES, (pp + 1) * LANES)
            q2 = _stack_masked(q_ref[:, sl], 2, HEAD_DIM).astype(BF16)
            ls, ms, masks = [], [], []
            for c in range(nc):
                z = _dot_nt(q2, k_ref[c * ck:(c + 1) * ck, sl].astype(BF16)) * (HEAD_DIM ** -0.5)
                l_c = _log_sigmoid(z)
                ls.append(l_c)
                if c < n_full:
                    ms.append(l_c - z)
                    masks.append(None)
                else:
                    mask = (c * ck + col) < qpos
                    ms.append(jnp.where(mask, l_c - z, 0.0))
                    masks.append(mask)
            carry = jnp.zeros((R, 1), F32)
            acc = jnp.zeros((R, LANES), F32)
            for c in range(nc - 1, -1, -1):
                suffix = _dot_hilo(ms[c], upper) + carry
                carry = carry + jnp.sum(ms[c], axis=1, keepdims=True)
                w = jnp.exp(ls[c] + suffix)
                if masks[c] is not None:
                    w = jnp.where(masks[c], w, 0.0)
                acc = acc + _dot(w.astype(BF16), v_ref[c * ck:(c + 1) * ck, sl].astype(BF16))
            outs.append(_pair_out(acc, tq))
        o_ref[...] = jnp.concatenate(outs, axis=1)

    _by_extent(i, tq, T, wb, run)


def _sb_prompt(Y, B, T, qc, kc, vc):
    tq = _pick(T, 128)
    wb = _extent_step(T)
    kern = functools.partial(_sb_prompt_kernel, tq=tq, T=T, wb=wb, ck=_pick(wb, 256))
    return _pair_call(kern, "sb_prompt", Y, B, T, tq, qc, kc, vc)


def _idx_scores(iq, ki_bf16, transposed=False):
    score = None
    w_off = IDX_HEADS * IDX_DIM + IDX_DIM
    for h in range(IDX_HEADS):
        qh = iq[:, h * IDX_DIM:(h + 1) * IDX_DIM].astype(BF16)
        s = _dot(qh, ki_bf16) if transposed else _dot_nt(qh, ki_bf16)
        term = jnp.maximum(s, 0.0) * iq[:, w_off + h:w_off + h + 1]
        score = term if score is None else score + term
    return score * (IDX_DIM ** -0.5)


def _dsa_prompt_kernel(q_ref, iq_ref, k_ref, v_ref, ik_ref, o_ref, *, tq, T, wb, ksel):
    i = pl.program_id(1)
    ki_off = IDX_HEADS * IDX_DIM

    def run(W):
        ki = ik_ref[0:W, ki_off:ki_off + IDX_DIM].astype(BF16)
        score = _idx_scores(iq_ref[...], ki)
        causal = _iota((tq, W), 1) <= _row_pos(i, tq, tq)
        sel = _topk_select(score, causal, ksel)
        sel2 = jnp.concatenate([sel, sel], axis=0)
        outs = []
        for p in range(SEC // LANES):
            sl = slice(p * LANES, (p + 1) * LANES)
            q2 = _stack_masked(q_ref[:, sl], 2, HEAD_DIM).astype(BF16)
            s_chunks = []
            for c in range(W // wb):
                ks = slice(c * wb, (c + 1) * wb)
                s = _dot_nt(q2, k_ref[ks, sl].astype(BF16)) * (HEAD_DIM ** -0.5)
                s_chunks.append(jnp.where(sel2[:, ks] > 0.0, s, NEG))
            o2 = _softmax_av(
                s_chunks, lambda pe, c: _dot(pe.astype(BF16), v_ref[c * wb:(c + 1) * wb, sl].astype(BF16)))
            outs.append(_pair_out(o2, tq))
        o_ref[...] = jnp.concatenate(outs, axis=1)

    _by_extent(i, tq, T, wb, run)


def _dsa_prompt(Y, B, T, qc, kc, vc, ic):
    tq = _pick(T, 128)
    nq = T // tq
    ksel = min(DSA_TOPK, T // 4)
    return pl.pallas_call(
        functools.partial(_dsa_prompt_kernel, tq=tq, T=T, wb=_extent_step(T), ksel=ksel),
        grid=(B, nq),
        in_specs=[
            pl.BlockSpec((tq, SEC), lambda b, i: (b * nq + i, qc)),
            pl.BlockSpec((tq, SEC), lambda b, i: (b * nq + i, ic)),
            pl.BlockSpec((T, SEC), lambda b, i: (b, kc)),
            pl.BlockSpec((T, SEC), lambda b, i: (b, vc)),
            pl.BlockSpec((T, SEC), lambda b, i: (b, ic)),
        ],
        out_specs=pl.BlockSpec((tq, SEC), lambda b, i: (b * nq + i, 0)),
        out_shape=jax.ShapeDtypeStruct((B * T, SEC), F32),
        compiler_params=_cparams("parallel", "arbitrary"),
        name="dsa_prompt",
    )(Y, Y, Y, Y, Y)


def _softmax_av(s_chunks, pv):
    mx = jnp.max(functools.reduce(jnp.maximum, s_chunks), axis=1, keepdims=True)
    lsum, acc = None, None
    for c, s in enumerate(s_chunks):
        p = jnp.exp(s - mx)
        a = pv(p, c)
        lsum = p if lsum is None else lsum + p
        acc = a if acc is None else acc + a
    return acc / jnp.sum(lsum, axis=1, keepdims=True)


def _diff_prompt_kernel(lam_ref, q_ref, k_ref, v_ref, gs_ref, o_ref, *, tq, T, wb, ck, out_scale):
    i = pl.program_id(2)
    R = 4 * tq
    qpos = _row_pos(i, tq, R)
    col = _iota((R, ck), 1)

    def run(W):
        n_full = (W - wb) // ck
        outs = []
        for pp in range(PAIRS_PER_STEP):
            sl = slice(pp * LANES, (pp + 1) * LANES)
            q4 = _stack_masked(q_ref[:, sl], 4, DIFF_DIM).astype(BF16)
            s_chunks = []
            for c in range(W // ck):
                s = _dot_nt(q4, k_ref[c * ck:(c + 1) * ck, sl].astype(BF16)) * (DIFF_DIM ** -0.5)
                s_chunks.append(s if c < n_full else jnp.where((c * ck + col) <= qpos, s, NEG))
            a = _softmax_av(
                s_chunks, lambda p, c: _dot(p.astype(BF16), v_ref[c * ck:(c + 1) * ck, sl].astype(BF16)))
            lam = lam_ref[...]
            low = _iota((tq, LANES), 1) < HEAD_DIM
            o = jnp.where(low, a[:tq] - lam * a[tq:2 * tq], a[2 * tq:3 * tq] - lam * a[3 * tq:])
            o2 = o * o
            s0 = jnp.sum(jnp.where(low, o2, 0.0), axis=1, keepdims=True)
            s1 = jnp.sum(jnp.where(low, 0.0, o2), axis=1, keepdims=True)
            ssq = jnp.where(low, s0, s1)
            outs.append(o * lax.rsqrt(ssq * (1.0 / HEAD_DIM) + EPS) * gs_ref[...] * out_scale)
        o_ref[...] = jnp.concatenate(outs, axis=1)

    _by_extent(i, tq, T, wb, run)


def _diff_prompt(Y, B, T, qc, kc, vc, lam, gsub, lam_init):
    tq = _pick(T, 128)
    wb = _extent_step(T)
    kern = functools.partial(_diff_prompt_kernel, tq=tq, T=T, wb=wb, ck=_pick(wb, 256),
                             out_scale=1.0 - lam_init)
    return _pair_call(
        kern, "diff_prompt", Y, B, T, tq, qc, kc, vc,
        lead=[(lam, pl.BlockSpec((1, 1), lambda b, p, i: (0, 0)))],
        tail=[(jnp.tile(gsub, 2).reshape(1, LANES), pl.BlockSpec((1, LANES), lambda b, p, i: (0, 0)))])


def _kmean_kernel(k_ref, o_ref):
    n = pl.program_id(1)

    @pl.when(n == 0)
    def _():
        o_ref[...] = jnp.zeros_like(o_ref)

    o_ref[0, pl.ds(n, 1), :] = jnp.sum(k_ref[...], axis=0, keepdims=True) * (1.0 / MOBA_BLOCK)


def _kmean(Y, B, T, kc, nbp):
    n_blk = T // MOBA_BLOCK
    return pl.pallas_call(
        _kmean_kernel,
        grid=(B, n_blk),
        in_specs=[pl.BlockSpec((MOBA_BLOCK, SEC), lambda b, n: (b * n_blk + n, kc))],
        out_specs=pl.BlockSpec((1, nbp, SEC), lambda b, n: (b, 0, 0)),
        out_shape=jax.ShapeDtypeStruct((B, nbp, SEC), F32),
        compiler_params=_cparams("parallel", "arbitrary"),
        name="kmean",
    )(Y)


def _moba_prompt_kernel(q_ref, km_ref, k_ref, v_ref, o_ref, *, tq, T, wb, topb):
    i = pl.program_id(2)
    R = 2 * tq
    ck = MOBA_BLOCK
    own = (i * tq) // MOBA_BLOCK
    qpos = _row_pos(i, tq, R)
    col = _iota((R, ck), 1)

    def run(W):
        n_full = (W - wb) // ck
        q2fs = [_stack_masked(q_ref[:, pp * LANES:(pp + 1) * LANES], 2, HEAD_DIM)
                for pp in range(PAIRS_PER_STEP)]
        gate = jnp.concatenate([_dot_nt_hilo2(q2fs[pp], km_ref[0, :, pp * LANES:(pp + 1) * LANES])
                                for pp in range(PAIRS_PER_STEP)], axis=0)
        n = _iota(gate.shape, 1)
        allowed_all = jnp.where(n == own, 1.0, _top_blocks(gate, n < own, topb))
        outs = []
        for pp in range(PAIRS_PER_STEP):
            sl = slice(pp * LANES, (pp + 1) * LANES)
            q2 = q2fs[pp].astype(BF16)
            allowed = allowed_all[pp * R:(pp + 1) * R]
            s_chunks = []
            for c in range(W // ck):
                s = _dot_nt(q2, k_ref[c * ck:(c + 1) * ck, sl].astype(BF16)) * (HEAD_DIM ** -0.5)
                ok = allowed[:, c:c + 1]
                if c >= n_full:
                    ok = jnp.where((c * ck + col) <= qpos, ok, 0.0)
                s_chunks.append(jnp.where(ok > 0.0, s, NEG))
            a = _softmax_av(
                s_chunks, lambda p, c: _dot(p.astype(BF16), v_ref[c * ck:(c + 1) * ck, sl].astype(BF16)))
            outs.append(_pair_out(a, tq))
        o_ref[...] = jnp.concatenate(outs, axis=1)

    _by_extent(i, tq, T, wb, run)


def _moba_prompt(Y, B, T, qc, kc, vc):
    assert T % MOBA_BLOCK == 0
    n_blk = T // MOBA_BLOCK
    nbp = -(-n_blk // LANES) * LANES
    topb = max(1, min(MOBA_TOPK, n_blk - 1))
    km = _kmean(Y, B, T, kc, nbp)
    tq = wb = MOBA_BLOCK
    gw = PAIRS_PER_STEP * LANES
    kern = functools.partial(_moba_prompt_kernel, tq=tq, T=T, wb=wb, topb=topb)
    return _pair_call(kern, "moba_prompt", Y, B, T, tq, qc, kc, vc,
                      mid=[(km, pl.BlockSpec((1, nbp, gw), lambda b, p, i: (b, 0, p)))])


def _cross_kernel(q_ref, mk_ref, mv_ref, o_ref):
    for h in range(MEM_HEADS):
        sl = slice(h * MEM_HEAD_DIM, (h + 1) * MEM_HEAD_DIM)
        s = _dot_nt(q_ref[:, sl].astype(BF16), mk_ref[:, sl].astype(BF16)) * (MEM_HEAD_DIM ** -0.5)
        mx = jnp.max(s, axis=1, keepdims=True)
        p = jnp.exp(s - mx)
        l = jnp.sum(p, axis=1, keepdims=True)
        o_ref[:, sl] = _dot(p.astype(BF16), mv_ref[:, sl].astype(BF16)) / l


def _cross_rows_kernel(q_ref, mk_ref, mv_ref, o_ref):
    mk = mk_ref[...].astype(BF16)
    mv = mv_ref[...].astype(BF16)
    head_of_col = _iota((q_ref.shape[0], mk.shape[0]), 1) % MEM_HEADS
    for h in range(MEM_HEADS):
        sl = slice(h * MEM_HEAD_DIM, (h + 1) * MEM_HEAD_DIM)
        s = _dot_nt(q_ref[:, sl].astype(BF16), mk) * (MEM_HEAD_DIM ** -0.5)
        s = jnp.where(head_of_col == h, s, NEG)
        p = jnp.exp(s - jnp.max(s, axis=1, keepdims=True))
        o_ref[:, sl] = _dot(p.astype(BF16), mv) / jnp.sum(p, axis=1, keepdims=True)


def _cross_rows(q, row0, B, Tq, mk, mv, rows, blk_of):
    r0 = row0 // Tq
    assert row0 % Tq == 0
    return pl.pallas_call(
        _cross_rows_kernel,
        grid=(B,),
        in_specs=[
            pl.BlockSpec((Tq, SEC), lambda b: (r0 + b, 0)),
            pl.BlockSpec((rows, MEM_HEAD_DIM), lambda b: (blk_of(b), 0)),
            pl.BlockSpec((rows, MEM_HEAD_DIM), lambda b: (blk_of(b), 0)),
        ],
        out_specs=pl.BlockSpec((Tq, SEC), lambda b: (b, 0)),
        out_shape=jax.ShapeDtypeStruct((B * Tq, SEC), F32),
        compiler_params=_cparams("parallel"),
        name="cross_rows",
    )(q, mk, mv)


def _cross(q, row0, B, Tq, mk, mv, mt, k_map, v_map):
    tq = _pick(Tq, 256)
    nq = Tq // tq
    r0 = row0 // tq
    assert row0 % tq == 0
    return pl.pallas_call(
        _cross_kernel,
        grid=(B, nq),
        in_specs=[
            pl.BlockSpec((tq, SEC), lambda b, i: (r0 + b * nq + i, 0)),
            pl.BlockSpec((mt, SEC), k_map),
            pl.BlockSpec((mt, SEC), v_map),
        ],
        out_specs=pl.BlockSpec((tq, SEC), lambda b, i: (b * nq + i, 0)),
        out_shape=jax.ShapeDtypeStruct((B * Tq, SEC), F32),
        compiler_params=_cparams("parallel", "arbitrary"),
        name="cross",
    )(q, mk, mv)


def _pad_rows(x, rows):
    return jnp.concatenate([x, jnp.zeros((rows - x.shape[0], x.shape[1]), x.dtype)], axis=0)


def _dec_positions(R, Ts, page, n_pages, past_len):
    col = _iota((R, page), 1)
    qpos = past_len + _iota((R, page), 0) % Ts
    kpos = [c * page + col for c in range(n_pages)] + [past_len + col]
    exists = [None] * n_pages + [col < Ts]
    return qpos, kpos, exists


def _diag_out(acc, n_heads, Ts, step=1):
    head = _iota((Ts, SEC), 1) // HEAD_DIM
    out = jnp.zeros((Ts, SEC), F32)
    for h in range(n_heads):
        r = step * h * Ts
        out = jnp.where(head == h, acc[r:r + Ts], out)
    return out


class _PagedKV:
    def __init__(self, kn_ref, vn_ref, kp, vp, page):
        self.kp, self.vp, self.page, self.n = kp, vp, page, len(kp)
        self.kn = _pad_rows(kn_ref[...], page).astype(BF16)
        self.vn = _pad_rows(vn_ref[...], page).astype(BF16)

    def kt(self, c):
        return self.kp[c][...].reshape(SEC, self.page)

    def scores(self, q_bf16, c):
        if c < self.n:
            return _dot(q_bf16, self.kt(c).astype(BF16))
        return _dot_nt(q_bf16, self.kn)

    def pv(self, p, c):
        if c < self.n:
            return _dot_nt(p.astype(BF16), self.vp[c][...].reshape(SEC, self.page).astype(BF16))
        return _dot(p.astype(BF16), self.vn)


def _softmax_chunks(logits, masks, kv):
    return _softmax_av([jnp.where(m, s, NEG) for s, m in zip(logits, masks)], kv.pv)


def _sb_dec_kernel(pt_ref, q_ref, kn_ref, vn_ref, *rest, n_pages, Ts, page, past_len):
    kp, vp = rest[:n_pages], rest[n_pages:2 * n_pages]
    H = SEC // HEAD_DIM
    R = H * Ts
    qbd = _stack_masked(q_ref[...], H, HEAD_DIM).astype(BF16)
    kv = _PagedKV(kn_ref, vn_ref, kp, vp, page)
    qpos, kpos, exists = _dec_positions(R, Ts, page, n_pages, past_len)
    upper = (_iota((page, page), 0) > _iota((page, page), 1)).astype(BF16)
    ls, ms, masks = [], [], []
    for c in range(n_pages + 1):
        z = kv.scores(qbd, c) * (HEAD_DIM ** -0.5)
        mask = kpos[c] < qpos
        if exists[c] is not None:
            mask = mask & exists[c]
        l_c = _log_sigmoid(z)
        ls.append(l_c)
        ms.append(jnp.where(mask, l_c - z, 0.0))
        masks.append(mask)
    run = jnp.zeros((R, 1), F32)
    acc = jnp.zeros((R, SEC), F32)
    for c in range(n_pages, -1, -1):
        suffix = _dot_hilo(ms[c], upper) + run
        run = run + jnp.sum(ms[c], axis=1, keepdims=True)
        w = jnp.where(masks[c], jnp.exp(ls[c] + suffix), 0.0)
        acc = acc + kv.pv(w, c)
    return _diag_out(acc, H, Ts)


def _dsa_dec_kernel(pt_ref, q_ref, iq_ref, kn_ref, vn_ref, *rest, n_pages, Ts, page, past_len, ksel):
    kp, vp = rest[:n_pages], rest[n_pages:2 * n_pages]
    ip = rest[2 * n_pages:3 * n_pages]
    H = SEC // HEAD_DIM
    iq = iq_ref[...]
    ki_off = IDX_HEADS * IDX_DIM
    chunks = [_idx_scores(iq, r[...].astype(BF16), transposed=True) for r in ip]
    chunks.append(_idx_scores(iq, _pad_rows(iq[:, ki_off:ki_off + IDX_DIM], page).astype(BF16)))
    score = jnp.concatenate(chunks, axis=1)
    kcol = _iota(score.shape, 1)
    valid = (kcol <= past_len + _iota(score.shape, 0)) & (kcol < past_len + Ts)
    sel = _topk_select(score, valid, ksel)
    qbd = _stack_masked(q_ref[...], H, HEAD_DIM).astype(BF16)
    kv = _PagedKV(kn_ref, vn_ref, kp, vp, page)
    logits, masks = [], []
    for c in range(n_pages + 1):
        logits.append(kv.scores(qbd, c) * (HEAD_DIM ** -0.5))
        sel_c = sel[:, c * page:(c + 1) * page]
        masks.append(jnp.concatenate([sel_c] * H, axis=0) > 0.0)
    return _diag_out(_softmax_chunks(logits, masks, kv), H, Ts)


def _diff_dec_kernel(pt_ref, lam_ref, q_ref, kn_ref, vn_ref, gs_ref, bd_ref, *rest,
                     n_pages, Ts, page, past_len, out_scale):
    kp, vp = rest[:n_pages], rest[n_pages:2 * n_pages]
    G = SEC // DIFF_DIM
    R = G * Ts
    qbd = _stack_masked(q_ref[...], G, DIFF_DIM).astype(BF16)
    kv = _PagedKV(kn_ref, vn_ref, kp, vp, page)
    qpos, kpos, exists = _dec_positions(R, Ts, page, n_pages, past_len)
    logits, masks = [], []
    for c in range(n_pages + 1):
        logits.append(kv.scores(qbd, c) * (DIFF_DIM ** -0.5))
        mask = kpos[c] <= qpos
        masks.append(mask if exists[c] is None else mask & exists[c])
    a = _softmax_chunks(logits, masks, kv)
    lam = lam_ref[...]
    d = jnp.concatenate([a[(2 * h) * Ts:(2 * h + 1) * Ts] - lam * a[(2 * h + 1) * Ts:(2 * h + 2) * Ts]
                         for h in range(G // 2)], axis=0)
    o = _diag_out(d, G // 2, Ts)
    ssq = _dot_hilo(o * o, bd_ref[...])
    return o * lax.rsqrt(ssq * (1.0 / HEAD_DIM) + EPS) * gs_ref[...] * out_scale


def _moba_dec_kernel(pt_ref, q_ref, kn_ref, vn_ref, *rest, n_pages, Ts, page, past_len, topb):
    kp, vp = rest[:n_pages], rest[n_pages:2 * n_pages]
    H = SEC // HEAD_DIM
    R = H * Ts
    ppb = MOBA_BLOCK // page
    own = past_len // MOBA_BLOCK
    qbdf = _stack_masked(q_ref[...], H, HEAD_DIM)
    qbd = qbdf.astype(BF16)
    kv = _PagedKV(kn_ref, vn_ref, kp, vp, page)
    nb_lane = _iota((SEC, LANES), 1)
    kmean_t = jnp.zeros((SEC, LANES), F32)
    for n in range(own):
        blk = functools.reduce(jnp.add, [kv.kt(c) for c in range(n * ppb, (n + 1) * ppb)])
        kmean_t = jnp.where(nb_lane == n, jnp.sum(blk, axis=1, keepdims=True) * (1.0 / MOBA_BLOCK), kmean_t)
    gate = _dot_hilo2(qbdf, kmean_t)
    sel = _top_blocks(gate, _iota((R, LANES), 1) < own, topb)
    qpos, kpos, exists = _dec_positions(R, Ts, page, n_pages, past_len)
    logits, masks = [], []
    for c in range(n_pages + 1):
        logits.append(kv.scores(qbd, c) * (HEAD_DIM ** -0.5))
        if c < n_pages:
            b = c // ppb
            masks.append(jnp.broadcast_to(sel[:, b:b + 1], (R, page)) > 0.0)
        else:
            masks.append((kpos[c] <= qpos) & exists[c])
    return _diag_out(_softmax_chunks(logits, masks, kv), H, Ts)


DEC_BATCH_PER_STEP = 2


def _dec_multi(pt_ref, *refs, kern, nb, n_lead, n_row, n_tail, per, Ts):
    lead = refs[:n_lead]
    rows = refs[n_lead:n_lead + n_row]
    base = n_lead + n_row + n_tail
    tail = refs[n_lead + n_row:base]
    o_ref = refs[base + nb * per]
    outs = []
    for s in range(nb):
        sl = pl.ds(s * Ts, Ts)
        outs.append(kern(pt_ref, *lead, *[r.at[sl, :] for r in rows], *tail,
                         *refs[base + s * per:base + (s + 1) * per]))
    o_ref[...] = jnp.concatenate(outs, axis=0)


def _dec_call(kern, name, page_table, Y, row0, Ts, cache_k, cache_v, layer, half, q_cols, extra=(),
              idx_cache=None, idx_layer=0):
    Bs, n_pages = page_table.shape
    page = cache_k.shape[4]
    hpb = SEC // HEAD_DIM
    nb = DEC_BATCH_PER_STEP if Bs % DEC_BATCH_PER_STEP == 0 and row0 % (DEC_BATCH_PER_STEP * Ts) == 0 else 1
    rb = row0 // (nb * Ts)

    def ysec(col):
        return pl.BlockSpec((nb * Ts, SEC), lambda g, pt, col=col: (rb + g, col))

    def pspec(s, p):
        return pl.BlockSpec((None, None, hpb, HEAD_DIM, page),
                            lambda g, pt, s=s, p=p: (pt[nb * g + s, p], layer, half, 0, 0))

    def ispec(s, p):
        return pl.BlockSpec((None, None, IDX_DIM, page),
                            lambda g, pt, s=s, p=p: (pt[nb * g + s, p], idx_layer, 0, 0))

    in_specs = []
    ops = []
    lead = [e for e in extra if e[0] == "lead"]
    tail = [e for e in extra if e[0] == "tail"]
    for _, arr, spec in lead:
        ops.append(arr)
        in_specs.append(spec)
    for col in q_cols:
        ops.append(Y)
        in_specs.append(ysec(col))
    for _, arr, spec in tail:
        ops.append(arr)
        in_specs.append(spec)
    for s in range(nb):
        for p in range(n_pages):
            ops.append(cache_k)
            in_specs.append(pspec(s, p))
        for p in range(n_pages):
            ops.append(cache_v)
            in_specs.append(pspec(s, p))
        if idx_cache is not None:
            for p in range(n_pages):
                ops.append(idx_cache)
                in_specs.append(ispec(s, p))
    per = n_pages * (2 if idx_cache is None else 3)
    multi = functools.partial(_dec_multi, kern=kern, nb=nb, n_lead=len(lead), n_row=len(q_cols),
                              n_tail=len(tail), per=per, Ts=Ts)
    return pl.pallas_call(
        multi,
        grid_spec=pltpu.PrefetchScalarGridSpec(
            num_scalar_prefetch=1,
            grid=(Bs // nb,),
            in_specs=in_specs,
            out_specs=pl.BlockSpec((nb * Ts, SEC), lambda g, pt: (g, 0)),
        ),
        out_shape=jax.ShapeDtypeStruct((Bs * Ts, SEC), F32),
        compiler_params=_cparams("arbitrary"),
        name=name,
    )(page_table, *ops)


def kernel(x_prompt, x_sample, mem_prompt, cache_k, cache_v, cache_idx_k, cache_mem_k, cache_mem_v, page_table, g_norm, w_ffn1_gu, w_ffn1_d, w_ffn2_gu, w_ffn2_d, w_in_even, g_qk_b, w_in_odd, g_qk_c, g_qk_d, g_sub_c, lambda_c, w_out, w_mem_q, w_mem_kv, w_mem_o, g_mem_qk):
    Bp, Tp, D = x_prompt.shape
    Bs, Ts, _ = x_sample.shape
    depth = g_norm.shape[0]
    n_phys, _, page, n_kv, hd = cache_k.shape
    n_pages = page_table.shape[1]
    past_len = n_pages * page
    mt = mem_prompt.shape[1]
    Mp, Ms = Bp * Tp, Bs * Ts
    assert D == 2 * SEC and n_kv * hd == D and hd == HEAD_DIM
    assert Mp % Ts == 0 and past_len % MOBA_BLOCK == 0 and MOBA_BLOCK % page == 0 and Ts <= page

    x = jnp.concatenate([x_prompt.reshape(Mp, D), x_sample.reshape(Ms, D)], axis=0)
    mem = mem_prompt.reshape(Bp * mt, D)
    pos = jnp.concatenate([jnp.tile(jnp.arange(Tp, dtype=I32), Bp),
                           jnp.tile(jnp.arange(Ts, dtype=I32) + past_len, Bs)])
    tab_a = _rope_tables(pos, HEAD_DIM, ROT_DIM)
    tab_b = _rope_tables(pos, DIFF_DIM, DIFF_ROT)
    ck = jnp.transpose(cache_k, (0, 1, 3, 4, 2))
    cv = jnp.transpose(cache_v, (0, 1, 3, 4, 2))
    cik = jnp.transpose(cache_idx_k, (0, 1, 3, 2))
    cmk = cache_mem_k.reshape(Bs * depth * mt * MEM_HEADS, MEM_HEAD_DIM)
    cmv = cache_mem_v.reshape(Bs * depth * mt * MEM_HEADS, MEM_HEAD_DIM)
    ones = jnp.ones((SEC,), F32)
    zeros = jnp.zeros((SEC,), F32)
    idx_w = IDX_HEADS * IDX_DIM + IDX_DIM + IDX_HEADS
    idx_on = (jnp.arange(SEC) < IDX_HEADS * IDX_DIM + IDX_DIM).astype(F32)
    ksel_s = min(DSA_TOPK, (past_len + Ts) // 4)
    topb_s = max(1, min(MOBA_TOPK, -(-(past_len + Ts) // MOBA_BLOCK) - 1))
    bd64 = _group_ones(HEAD_DIM)

    Ys, mkvs = [], []
    for l in range(depth):
        g = g_norm[l]
        x = _ffn(x, g[0], w_ffn1_gu[l].astype(BF16), w_ffn1_d[l].astype(BF16))
        l2 = l // 2
        if l % 2 == 0:
            w = w_in_even[l2]
            wcols = [w[:, 0:512], w[:, 1536:2048], w[:, 512:1024], w[:, 2048:2560], w[:, 1024:1536],
                     w[:, 2560:3072], jnp.pad(w[:, 3072:3072 + idx_w], ((0, 0), (0, SEC - idx_w)))]
            cfgs = [(0, -1), (HEAD_DIM, 0), (0, -1), (HEAD_DIM, 0), (0, -1), (0, -1), (0, 0)]
            gains = [ones, jnp.tile(g_qk_b[l2, 0], 8), ones, jnp.tile(g_qk_b[l2, 1], 8), ones, ones, ones]
            rmask = [zeros, ones, zeros, ones, zeros, zeros, idx_on]
            tabs, ropes = list(tab_a), [(HEAD_DIM, ROT_DIM // 2)]
        else:
            w = w_in_odd[l2]
            wcols = [w[:, 0:512], w[:, 1536:2048], w[:, 512:1024], w[:, 2048:2560], w[:, 1024:1536],
                     w[:, 2560:3072]]
            cfgs = [(DIFF_DIM, 1), (HEAD_DIM, 0), (DIFF_DIM, 1), (HEAD_DIM, 0), (0, -1), (0, -1)]
            gains = [jnp.tile(g_qk_c[l2, 0], 8), jnp.tile(g_qk_d[l2, 0], 8), jnp.tile(g_qk_c[l2, 1], 8),
                     jnp.tile(g_qk_d[l2, 1], 8), ones, ones]
            rmask = [ones, ones, ones, ones, zeros, zeros]
            tabs = list(tab_a) + list(tab_b)
            ropes = [(HEAD_DIM, ROT_DIM // 2), (DIFF_DIM, DIFF_ROT // 2)]
        Y = _proj(x, g[1], jnp.concatenate(wcols, axis=1).astype(BF16),
                  jnp.concatenate(gains).reshape(1, -1), jnp.concatenate(rmask).reshape(1, -1),
                  cfgs, tabs, ropes)
        Ys.append(Y)

        if l % 2 == 0:
            oa_p = _sb_prompt(Y, Bp, Tp, 0, 2, 4)
            ob_p = _dsa_prompt(Y, Bp, Tp, 1, 3, 5, 6)
            oa_s = _dec_call(
                functools.partial(_sb_dec_kernel, n_pages=n_pages, Ts=Ts, page=page, past_len=past_len),
                "sb_dec", page_table, Y, Mp, Ts, ck, cv, l, 0, [0, 2, 4])
            ob_s = _dec_call(
                functools.partial(_dsa_dec_kernel, n_pages=n_pages, Ts=Ts, page=page, past_len=past_len,
                                  ksel=ksel_s),
                "dsa_dec", page_table, Y, Mp, Ts, ck, cv, l, 1, [1, 6, 3, 5],
                idx_cache=cik, idx_layer=l2)
        else:
            lam_init = 0.8 - 0.6 * math.exp(-0.3 * l)
            lp = lambda_c[l2].astype(F32)
            lam = (jnp.exp(jnp.sum(lp[0] * lp[1])) - jnp.exp(jnp.sum(lp[2] * lp[3])) + lam_init).reshape(1, 1)
            oa_p = _diff_prompt(Y, Bp, Tp, 0, 2, 4, lam, g_sub_c[l2], lam_init)
            ob_p = _moba_prompt(Y, Bp, Tp, 1, 3, 5)
            one = pl.BlockSpec((1, 1), lambda b, pt: (0, 0))
            oa_s = _dec_call(
                functools.partial(_diff_dec_kernel, n_pages=n_pages, Ts=Ts, page=page, past_len=past_len,
                                  out_scale=1.0 - lam_init),
                "diff_dec", page_table, Y, Mp, Ts, ck, cv, l, 0, [0, 2, 4],
                extra=[("lead", lam, one),
                       ("tail", jnp.tile(g_sub_c[l2], 8).reshape(1, SEC),
                        pl.BlockSpec((1, SEC), lambda b, pt: (0, 0))),
                       ("tail", bd64, pl.BlockSpec((SEC, SEC), lambda b, pt: (0, 0)))])
            ob_s = _dec_call(
                functools.partial(_moba_dec_kernel, n_pages=n_pages, Ts=Ts, page=page, past_len=past_len,
                                  topb=topb_s),
                "moba_dec", page_table, Y, Mp, Ts, ck, cv, l, 1, [1, 3, 5])
        o = jnp.concatenate([jnp.concatenate([oa_p, ob_p], axis=1),
                             jnp.concatenate([oa_s, ob_s], axis=1)], axis=0)
        x = _matres(o, w_out[l].astype(BF16), x)

        g_mq = jnp.tile(g_mem_qk[l, 0], MEM_HEADS).reshape(1, SEC)
        qm = _proj(x, g[2], w_mem_q[l].astype(BF16), g_mq, jnp.zeros((1, SEC), F32),
                   [(MEM_HEAD_DIM, -1)], [], [])
        g_mk = jnp.concatenate([jnp.tile(g_mem_qk[l, 1], MEM_HEADS), ones]).reshape(1, 2 * SEC)
        mkv = _proj(mem, g[4], w_mem_kv[l].astype(BF16), g_mk, jnp.zeros((1, 2 * SEC), F32),
                    [(MEM_HEAD_DIM, -1), (0, -1)], [], [])
        mkvs.append(mkv)
        oc_p = _cross(qm, 0, Bp, Tp, mkv, mkv, mt, lambda b, i: (b, 0), lambda b, i: (b, 1))
        oc_s = _cross_rows(qm, Mp, Bs, Ts, cmk, cmv, mt * MEM_HEADS, lambda b, l=l: b * depth + l)
        x = _matres(jnp.concatenate([oc_p, oc_s], axis=0), w_mem_o[l].astype(BF16), x)
        x = _ffn(x, g[3], w_ffn2_gu[l].astype(BF16), w_ffn2_d[l].astype(BF16))

    def stack(rows0, rows1, col0, col1, shape, arrs):
        return jnp.stack([a[rows0:rows1, col0:col1].reshape(shape) for a in arrs], axis=1)

    ki0 = 6 * SEC + IDX_HEADS * IDX_DIM
    y_prompt = x[:Mp].reshape(Bp, Tp, D)
    y_sample = x[Mp:].reshape(Bs, Ts, D)
    k_prompt = stack(0, Mp, 2 * SEC, 4 * SEC, (Bp, Tp, n_kv, hd), Ys)
    v_prompt = stack(0, Mp, 4 * SEC, 6 * SEC, (Bp, Tp, n_kv, hd), Ys)
    idx_k_prompt = stack(0, Mp, ki0, ki0 + IDX_DIM, (Bp, Tp, IDX_DIM), Ys[0::2])
    mem_k_prompt = stack(0, Bp * mt, 0, SEC, (Bp, mt, MEM_HEADS, MEM_HEAD_DIM), mkvs)
    mem_v_prompt = stack(0, Bp * mt, SEC, 2 * SEC, (Bp, mt, MEM_HEADS, MEM_HEAD_DIM), mkvs)
    k_sample = stack(Mp, Mp + Ms, 2 * SEC, 4 * SEC, (Bs, Ts, n_kv, hd), Ys)
    v_sample = stack(Mp, Mp + Ms, 4 * SEC, 6 * SEC, (Bs, Ts, n_kv, hd), Ys)
    idx_k_sample = stack(Mp, Mp + Ms, ki0, ki0 + IDX_DIM, (Bs, Ts, IDX_DIM), Ys[0::2])
    return (y_prompt, y_sample, k_prompt, v_prompt, idx_k_prompt, mem_k_prompt, mem_v_prompt,
            k_sample, v_sample, idx_k_sample)
```

```python
import functools
import math

import jax
import jax.numpy as jnp
from jax import lax
from jax.experimental import pallas as pl
from jax.experimental.pallas import tpu as pltpu

F32 = jnp.float32
BF16 = jnp.bfloat16
I32 = jnp.int32

EPS = 1e-6
HEAD_DIM = 64
DIFF_DIM = 32
ROT_DIM = 16
DIFF_ROT = 8
IDX_HEADS = 4
IDX_DIM = 64
DSA_TOPK = 256
MOBA_BLOCK = 256
MOBA_TOPK = 3
ROPE_THETA = 500000.0
MEM_HEADS = 4
MEM_HEAD_DIM = 128

LANES = 128
SEC = 512
NEG = -1e30
VMEM_LIMIT = 56 * 1024 * 1024


def _cparams(*sem):
    return pltpu.CompilerParams(dimension_semantics=sem, vmem_limit_bytes=VMEM_LIMIT)


def _pick(n, target):
    t = min(n, target)
    while t > 8 and (n % t or t % 8):
        t -= 1
    assert n % t == 0, (n, target)
    return t


def _dot(a, b):
    return jnp.dot(a, b, preferred_element_type=F32)


def _dot_nt(a, b):
    return lax.dot_general(a, b, (((1,), (1,)), ((), ())), preferred_element_type=F32)


def _split(x):
    hi = x.astype(BF16)
    lo = (x - hi.astype(F32)).astype(BF16)
    return hi, lo


def _dot_hilo(x, w_bf16):
    hi, lo = _split(x)
    return _dot(hi, w_bf16) + _dot(lo, w_bf16)


def _dot_nt_hilo2(a, b):
    ah, al = _split(a)
    bh, bl = _split(b)
    return _dot_nt(ah, bh) + _dot_nt(ah, bl) + _dot_nt(al, bh)


def _dot_hilo2(a, b):
    ah, al = _split(a)
    bh, bl = _split(b)
    return _dot(ah, bh) + _dot(ah, bl) + _dot(al, bh)


def _iota(shape, dim):
    return lax.broadcasted_iota(I32, shape, dim)


def _rms_rows(x, g):
    return x * lax.rsqrt(jnp.mean(x * x, axis=-1, keepdims=True) + EPS) * g


def _ffn_kernel(x_ref, g_ref, wg_ref, wu_ref, wd_ref, o_ref, h_scr, acc_scr):
    f = pl.program_id(1)

    @pl.when(f == 0)
    def _():
        h_scr[...] = _rms_rows(x_ref[...], g_ref[...]).astype(BF16)
        acc_scr[...] = jnp.zeros_like(acc_scr)

    h = h_scr[...]
    g = _dot(h, wg_ref[...])
    u = _dot(h, wu_ref[...])
    a = (g * (1.0 / (1.0 + jnp.exp(-g)))) * u
    acc_scr[...] += _dot(a.astype(BF16), wd_ref[...])

    @pl.when(f == pl.num_programs(1) - 1)
    def _():
        o_ref[...] = x_ref[...] + 0.5 * acc_scr[...]


def _ffn(x, g, w_gu, w_d):
    M, D = x.shape
    F = w_d.shape[0]
    tm = _pick(M, 1024)
    tf = _pick(F, 256)
    nf = F // tf
    return pl.pallas_call(
        _ffn_kernel,
        grid=(M // tm, nf),
        in_specs=[
            pl.BlockSpec((tm, D), lambda i, f: (i, 0)),
            pl.BlockSpec((1, D), lambda i, f: (0, 0)),
            pl.BlockSpec((D, tf), lambda i, f: (0, f)),
            pl.BlockSpec((D, tf), lambda i, f: (0, nf + f)),
            pl.BlockSpec((tf, D), lambda i, f: (f, 0)),
        ],
        out_specs=pl.BlockSpec((tm, D), lambda i, f: (i, 0)),
        out_shape=jax.ShapeDtypeStruct((M, D), F32),
        scratch_shapes=[pltpu.VMEM((tm, D), BF16), pltpu.VMEM((tm, D), F32)],
        compiler_params=_cparams("parallel", "arbitrary"),
        name="ffn",
    )(x, g.reshape(1, D), w_gu, w_gu, w_d)


def _group_ones(gs):
    r = jnp.arange(SEC) // gs
    return (r[:, None] == r[None, :]).astype(BF16)


def _rope_tables(pos, gs, rot):
    half = rot // 2
    freq = ROPE_THETA ** (-jnp.arange(half, dtype=F32) * 2.0 / rot)
    ang = pos.astype(F32)[:, None] * freq[None, :]
    cos, sin = jnp.cos(ang), jnp.sin(ang)
    d = jnp.arange(LANES) % gs
    c_full = cos[:, d % half]
    s_full = sin[:, d % half]
    C = jnp.where(d[None, :] < rot, c_full, 1.0)
    S = jnp.where(d[None, :] < half, -s_full, jnp.where(d[None, :] < rot, s_full, 0.0))
    return C, S


def _proj_kernel(*refs, cfgs, ropes, group_sizes, tn):
    x_ref, g_ref, w_ref, gain_ref, rmask_ref = refs[:5]
    n_tab = 2 * len(ropes)
    tab_refs = refs[5:5 + n_tab]
    bd_refs = refs[5 + n_tab:5 + n_tab + len(group_sizes)]
    o_ref, h_scr = refs[5 + n_tab + len(group_sizes):]
    j = pl.program_id(1)

    @pl.when(j == 0)
    def _():
        h_scr[...] = _rms_rows(x_ref[...], g_ref[...]).astype(BF16)

    y = _dot(h_scr[...], w_ref[...])
    tm = y.shape[0]

    def epilogue(y, cfg):
        gs, rope = cfg
        if gs:
            bd = bd_refs[group_sizes.index(gs)][...]
            ssq = _dot_hilo(y * y, bd)
            y = y * lax.rsqrt(ssq * (1.0 / gs) + EPS) * gain_ref[...]
        if rope < 0:
            o_ref[...] = y
            return
        period, half = ropes[rope]
        C = tab_refs[2 * rope][...]
        S = tab_refs[2 * rope + 1][...]
        first = (_iota((tm, LANES), 1) % period) < half
        for c in range(tn // LANES):
            sl = slice(c * LANES, (c + 1) * LANES)
            yc = y[:, sl]
            on = rmask_ref[:, sl] > 0.0
            partner = jnp.where(first, pltpu.roll(yc, LANES - half, 1), pltpu.roll(yc, half, 1))
            o_ref[:, sl] = yc * jnp.where(on, C, 1.0) + partner * jnp.where(on, S, 0.0)

    for cfg in sorted(set(cfgs)):
        cond = None
        for s, c in enumerate(cfgs):
            if c == cfg:
                cond = (j == s) if cond is None else (cond | (j == s))
        pl.when(cond)(functools.partial(epilogue, y, cfg))


def _proj(x, gnorm, w, gain, rmask, cfgs, tabs, ropes):
    M, D = x.shape
    N = w.shape[1]
    nsec = N // SEC
    assert nsec == len(cfgs)
    tm = _pick(M, 512)
    group_sizes = tuple(sorted({c[0] for c in cfgs if c[0]}))
    bds = [_group_ones(gs) for gs in group_sizes]
    kern = functools.partial(_proj_kernel, cfgs=tuple(cfgs), ropes=tuple(ropes),
                             group_sizes=group_sizes, tn=SEC)
    in_specs = [
        pl.BlockSpec((tm, D), lambda i, j: (i, 0)),
        pl.BlockSpec((1, D), lambda i, j: (0, 0)),
        pl.BlockSpec((D, SEC), lambda i, j: (0, j)),
        pl.BlockSpec((1, SEC), lambda i, j: (0, j)),
        pl.BlockSpec((1, SEC), lambda i, j: (0, j)),
    ]
    in_specs += [pl.BlockSpec((tm, LANES), lambda i, j: (i, 0)) for _ in tabs]
    in_specs += [pl.BlockSpec((SEC, SEC), lambda i, j: (0, 0)) for _ in bds]
    return pl.pallas_call(
        kern,
        grid=(M // tm, nsec),
        in_specs=in_specs,
        out_specs=pl.BlockSpec((tm, SEC), lambda i, j: (i, j)),
        out_shape=jax.ShapeDtypeStruct((M, N), F32),
        scratch_shapes=[pltpu.VMEM((tm, D), BF16)],
        compiler_params=_cparams("parallel", "arbitrary"),
        name="proj",
    )(x, gnorm.reshape(1, D), w, gain, rmask, *tabs, *bds)


def _matres_kernel(a_ref, w_ref, x_ref, o_ref):
    o_ref[...] = x_ref[...] + _dot(a_ref[...].astype(BF16), w_ref[...])


def _matres(a, w, x):
    M, K = a.shape
    N = w.shape[1]
    tm = _pick(M, 1024)
    return pl.pallas_call(
        _matres_kernel,
        grid=(M // tm,),
        in_specs=[
            pl.BlockSpec((tm, K), lambda i: (i, 0)),
            pl.BlockSpec((K, N), lambda i: (0, 0)),
            pl.BlockSpec((tm, N), lambda i: (i, 0)),
        ],
        out_specs=pl.BlockSpec((tm, N), lambda i: (i, 0)),
        out_shape=jax.ShapeDtypeStruct((M, N), F32),
        compiler_params=_cparams("parallel"),
        name="matres",
    )(a, w, x)


def _stack_masked(q, n_groups, width):
    t, L = q.shape
    lane_grp = _iota((t, L), 1) // width
    return jnp.concatenate([jnp.where(lane_grp == r, q, 0.0) for r in range(n_groups)], axis=0)


def _log_sigmoid(z):
    return jnp.minimum(z, 0.0) - jnp.log(1.0 + jnp.exp(-jnp.abs(z)))


def _topk_select(score, valid, ksel, digit_bits=1):
    R, L = score.shape
    score = jnp.where(valid, score, -jnp.inf)
    bits = lax.bitcast_convert_type(score, I32)
    key = jnp.where(bits < 0, bits ^ jnp.int32(0x7FFFFFFF), bits)
    kf = jnp.float32(ksel)

    def count_ge(t):
        return jnp.sum((key >= t).astype(F32), axis=1, keepdims=True)

    int_min = jnp.int32(-2 ** 31)
    thr = jnp.where(count_ge(jnp.zeros((R, 1), I32)) >= kf, jnp.int32(0), int_min)

    if digit_bits == 1:
        def body(it, thr):
            cand = thr | lax.shift_left(jnp.int32(1), 30 - it)
            return jnp.where(count_ge(cand) >= kf, cand, thr)

        thr = lax.fori_loop(0, 31, body, thr)
    else:
        hi = 31
        while hi > 0:
            nbits = min(digit_bits, hi)
            shift = hi - nbits
            cands = [thr | jnp.int32(d << shift) for d in range(1, 2 ** nbits)]
            ge = jnp.concatenate([(key >= c).astype(F32) for c in cands], axis=0)
            cnt = jnp.sum(ge, axis=1, keepdims=True)
            digit = functools.reduce(
                jnp.add, [jnp.where(cnt[j * R:(j + 1) * R] >= kf, 1, 0) for j in range(len(cands))])
            thr = thr | lax.shift_left(digit.astype(I32), shift)
            hi = shift
    need = kf - jnp.sum((key > thr).astype(F32), axis=1, keepdims=True)
    validf = jnp.where(valid, 1.0, 0.0)
    incl = (_iota((LANES, LANES), 0) <= _iota((LANES, LANES), 1)).astype(BF16)
    run = jnp.zeros((R, 1), F32)
    outs = []
    for c in range(L // LANES):
        sl = slice(c * LANES, (c + 1) * LANES)
        key_c = key[:, sl]
        eqf = jnp.where(key_c == thr, 1.0, 0.0)
        rank = _dot(eqf.astype(BF16), incl) + run
        run = run + jnp.sum(eqf, axis=1, keepdims=True)
        take = jnp.where(key_c > thr, 1.0, jnp.where(rank <= need, eqf, 0.0))
        outs.append(take * validf[:, sl])
    return jnp.concatenate(outs, axis=1)


def _top_blocks(gate, ok, topb):
    R, NB = gate.shape
    n = _iota((R, NB), 1)
    g = jnp.where(ok, gate, -jnp.inf)
    sel = jnp.zeros((R, NB), F32)
    for _ in range(topb):
        mx = jnp.max(g, axis=1, keepdims=True)
        idx = jnp.min(jnp.where(g == mx, n, NB), axis=1, keepdims=True)
        pick = n == jnp.where(mx > -jnp.inf, idx, -1)
        sel = jnp.where(pick, 1.0, sel)
        g = jnp.where(pick, -jnp.inf, g)
    return sel


def _pair_out(a, tq):
    lane = _iota((tq, LANES), 1)
    return jnp.where(lane < HEAD_DIM, a[:tq], a[tq:])


def _by_extent(i, tq, T, wb, fn):
    need = ((i + 1) * tq + wb - 1) // wb
    for v in range(1, T // wb + 1):
        pl.when(need == v)(functools.partial(fn, v * wb))


def _extent_step(T):
    return _pick(T, max(T // 8, 256))


PAIRS_PER_STEP = 2


def _row_pos(i, tq, rows):
    return i * tq + _iota((rows, 1), 0) % tq


def _pair_call(kern, name, Y, B, T, tq, qc, kc, vc, lead=(), mid=(), tail=()):
    nq = T // tq
    gw = PAIRS_PER_STEP * LANES
    ng = SEC // gw
    in_specs = [s for _, s in lead]
    in_specs.append(pl.BlockSpec((tq, gw), lambda b, p, i: (b * nq + i, qc * ng + p)))
    in_specs += [s for _, s in mid]
    in_specs.append(pl.BlockSpec((T, gw), lambda b, p, i: (b, kc * ng + p)))
    in_specs.append(pl.BlockSpec((T, gw), lambda b, p, i: (b, vc * ng + p)))
    in_specs += [s for _, s in tail]
    ops = [a for a, _ in lead] + [Y] + [a for a, _ in mid] + [Y, Y] + [a for a, _ in tail]
    return pl.pallas_call(
        kern,
        grid=(B, ng, nq),
        in_specs=in_specs,
        out_specs=pl.BlockSpec((tq, gw), lambda b, p, i: (b * nq + i, p)),
        out_shape=jax.ShapeDtypeStruct((B * T, SEC), F32),
        compiler_params=_cparams("parallel", "parallel", "arbitrary"),
        name=name,
    )(*ops)


def _sb_prompt_kernel(q_ref, k_ref, v_ref, o_ref, *, tq, T, wb, ck):
    i = pl.program_id(2)
    R = 2 * tq
    upper = (_iota((ck, ck), 0) > _iota((ck, ck), 1)).astype(BF16)
    qpos = _row_pos(i, tq, R)
    col = _iota((R, ck), 1)

    def run(W):
        nc = W // ck
        n_full = (W - wb) // ck
        outs = []
        for pp in range(PAIRS_PER_STEP):
            sl = slice(pp * LANES, (pp + 1) * LANES)
            q2 = _stack_masked(q_ref[:, sl], 2, HEAD_DIM).astype(BF16)
            ls, ms, masks = [], [], []
            for c in range(nc):
                z = _dot_nt(q2, k_ref[c * ck:(c + 1) * ck, sl].astype(BF16)) * (HEAD_DIM ** -0.5)
                l_c = _log_sigmoid(z)
                ls.append(l_c)
                if c < n_full:
                    ms.append(l_c - z)
                    masks.append(None)
                else:
                    mask = (c * ck + col) < qpos
                    ms.append(jnp.where(mask, l_c - z, 0.0))
                    masks.append(mask)
            carry = jnp.zeros((R, 1), F32)
            acc = jnp.zeros((R, LANES), F32)
            for c in range(nc - 1, -1, -1):
                suffix = _dot_hilo(ms[c], upper) + carry
                carry = carry + jnp.sum(ms[c], axis=1, keepdims=True)
                w = jnp.exp(ls[c] + suffix)
                if masks[c] is not None:
                    w = jnp.where(masks[c], w, 0.0)
                acc = acc + _dot(w.astype(BF16), v_ref[c * ck:(c + 1) * ck, sl].astype(BF16))
            outs.append(_pair_out(acc, tq))
        o_ref[...] = jnp.concatenate(outs, axis=1)

    _by_extent(i, tq, T, wb, run)


def _sb_prompt(Y, B, T, qc, kc, vc):
    tq = _pick(T, 128)
    wb = _extent_step(T)
    kern = functools.partial(_sb_prompt_kernel, tq=tq, T=T, wb=wb, ck=_pick(wb, 256))
    return _pair_call(kern, "sb_prompt", Y, B, T, tq, qc, kc, vc)


def _idx_scores(iq, ki_bf16, transposed=False):
    score = None
    w_off = IDX_HEADS * IDX_DIM + IDX_DIM
    for h in range(IDX_HEADS):
        qh = iq[:, h * IDX_DIM:(h + 1) * IDX_DIM].astype(BF16)
        s = _dot(qh, ki_bf16) if transposed else _dot_nt(qh, ki_bf16)
        term = jnp.maximum(s, 0.0) * iq[:, w_off + h:w_off + h + 1]
        score = term if score is None else score + term
    return score * (IDX_DIM ** -0.5)


def _dsa_prompt_kernel(q_ref, iq_ref, k_ref, v_ref, ik_ref, o_ref, *, tq, T, wb, ksel):
    i = pl.program_id(1)
    ki_off = IDX_HEADS * IDX_DIM

    def run(W):
        ki = ik_ref[0:W, ki_off:ki_off + IDX_DIM].astype(BF16)
        score = _idx_scores(iq_ref[...], ki)
        causal = _iota((tq, W), 1) <= _row_pos(i, tq, tq)
        sel = _topk_select(score, causal, ksel)
        sel2 = jnp.concatenate([sel, sel], axis=0)
        outs = []
        for p in range(SEC // LANES):
            sl = slice(p * LANES, (p + 1) * LANES)
            q2 = _stack_masked(q_ref[:, sl], 2, HEAD_DIM).astype(BF16)
            s_chunks = []
            for c in range(W // wb):
                ks = slice(c * wb, (c + 1) * wb)
                s = _dot_nt(q2, k_ref[ks, sl].astype(BF16)) * (HEAD_DIM ** -0.5)
                s_chunks.append(jnp.where(sel2[:, ks] > 0.0, s, NEG))
            o2 = _softmax_av(
                s_chunks, lambda pe, c: _dot(pe.astype(BF16), v_ref[c * wb:(c + 1) * wb, sl].astype(BF16)))
            outs.append(_pair_out(o2, tq))
        o_ref[...] = jnp.concatenate(outs, axis=1)

    _by_extent(i, tq, T, wb, run)


def _dsa_prompt(Y, B, T, qc, kc, vc, ic):
    tq = _pick(T, 128)
    nq = T // tq
    ksel = min(DSA_TOPK, T // 4)
    return pl.pallas_call(
        functools.partial(_dsa_prompt_kernel, tq=tq, T=T, wb=_extent_step(T), ksel=ksel),
        grid=(B, nq),
        in_specs=[
            pl.BlockSpec((tq, SEC), lambda b, i: (b * nq + i, qc)),
            pl.BlockSpec((tq, SEC), lambda b, i: (b * nq + i, ic)),
            pl.BlockSpec((T, SEC), lambda b, i: (b, kc)),
            pl.BlockSpec((T, SEC), lambda b, i: (b, vc)),
            pl.BlockSpec((T, SEC), lambda b, i: (b, ic)),
        ],
        out_specs=pl.BlockSpec((tq, SEC), lambda b, i: (b * nq + i, 0)),
        out_shape=jax.ShapeDtypeStruct((B * T, SEC), F32),
        compiler_params=_cparams("parallel", "arbitrary"),
        name="dsa_prompt",
    )(Y, Y, Y, Y, Y)


def _softmax_av(s_chunks, pv):
    mx = jnp.max(functools.reduce(jnp.maximum, s_chunks), axis=1, keepdims=True)
    lsum, acc = None, None
    for c, s in enumerate(s_chunks):
        p = jnp.exp(s - mx)
        a = pv(p, c)
        lsum = p if lsum is None else lsum + p
        acc = a if acc is None else acc + a
    return acc / jnp.sum(lsum, axis=1, keepdims=True)


def _diff_prompt_kernel(lam_ref, q_ref, k_ref, v_ref, gs_ref, o_ref, *, tq, T, wb, ck, out_scale):
    i = pl.program_id(2)
    R = 4 * tq
    qpos = _row_pos(i, tq, R)
    col = _iota((R, ck), 1)

    def run(W):
        n_full = (W - wb) // ck
        outs = []
        for pp in range(PAIRS_PER_STEP):
            sl = slice(pp * LANES, (pp + 1) * LANES)
            q4 = _stack_masked(q_ref[:, sl], 4, DIFF_DIM).astype(BF16)
            s_chunks = []
            for c in range(W // ck):
                s = _dot_nt(q4, k_ref[c * ck:(c + 1) * ck, sl].astype(BF16)) * (DIFF_DIM ** -0.5)
                s_chunks.append(s if c < n_full else jnp.where((c * ck + col) <= qpos, s, NEG))
            a = _softmax_av(
                s_chunks, lambda p, c: _dot(p.astype(BF16), v_ref[c * ck:(c + 1) * ck, sl].astype(BF16)))
            lam = lam_ref[...]
            low = _iota((tq, LANES), 1) < HEAD_DIM
            o = jnp.where(low, a[:tq] - lam * a[tq:2 * tq], a[2 * tq:3 * tq] - lam * a[3 * tq:])
            o2 = o * o
            s0 = jnp.sum(jnp.where(low, o2, 0.0), axis=1, keepdims=True)
            s1 = jnp.sum(jnp.where(low, 0.0, o2), axis=1, keepdims=True)
            ssq = jnp.where(low, s0, s1)
            outs.append(o * lax.rsqrt(ssq * (1.0 / HEAD_DIM) + EPS) * gs_ref[...] * out_scale)
        o_ref[...] = jnp.concatenate(outs, axis=1)

    _by_extent(i, tq, T, wb, run)


def _diff_prompt(Y, B, T, qc, kc, vc, lam, gsub, lam_init):
    tq = _pick(T, 128)
    wb = _extent_step(T)
    kern = functools.partial(_diff_prompt_kernel, tq=tq, T=T, wb=wb, ck=_pick(wb, 256),
                             out_scale=1.0 - lam_init)
    return _pair_call(
        kern, "diff_prompt", Y, B, T, tq, qc, kc, vc,
        lead=[(lam, pl.BlockSpec((1, 1), lambda b, p, i: (0, 0)))],
        tail=[(jnp.tile(gsub, 2).reshape(1, LANES), pl.BlockSpec((1, LANES), lambda b, p, i: (0, 0)))])


def _kmean_kernel(k_ref, o_ref):
    n = pl.program_id(1)

    @pl.when(n == 0)
    def _():
        o_ref[...] = jnp.zeros_like(o_ref)

    o_ref[0, pl.ds(n, 1), :] = jnp.sum(k_ref[...], axis=0, keepdims=True) * (1.0 / MOBA_BLOCK)


def _kmean(Y, B, T, kc, nbp):
    n_blk = T // MOBA_BLOCK
    return pl.pallas_call(
        _kmean_kernel,
        grid=(B, n_blk),
        in_specs=[pl.BlockSpec((MOBA_BLOCK, SEC), lambda b, n: (b * n_blk + n, kc))],
        out_specs=pl.BlockSpec((1, nbp, SEC), lambda b, n: (b, 0, 0)),
        out_shape=jax.ShapeDtypeStruct((B, nbp, SEC), F32),
        compiler_params=_cparams("parallel", "arbitrary"),
        name="kmean",
    )(Y)


def _moba_prompt_kernel(q_ref, km_ref, k_ref, v_ref, o_ref, *, tq, T, wb, topb):
    i = pl.program_id(2)
    R = 2 * tq
    ck = MOBA_BLOCK
    own = (i * tq) // MOBA_BLOCK
    qpos = _row_pos(i, tq, R)
    col = _iota((R, ck), 1)

    def run(W):
        n_full = (W - wb) // ck
        q2fs = [_stack_masked(q_ref[:, pp * LANES:(pp + 1) * LANES], 2, HEAD_DIM)
                for pp in range(PAIRS_PER_STEP)]
        gate = jnp.concatenate([_dot_nt_hilo2(q2fs[pp], km_ref[0, :, pp * LANES:(pp + 1) * LANES])
                                for pp in range(PAIRS_PER_STEP)], axis=0)
        n = _iota(gate.shape, 1)
        allowed_all = jnp.where(n == own, 1.0, _top_blocks(gate, n < own, topb))
        outs = []
        for pp in range(PAIRS_PER_STEP):
            sl = slice(pp * LANES, (pp + 1) * LANES)
            q2 = q2fs[pp].astype(BF16)
            allowed = allowed_all[pp * R:(pp + 1) * R]
            s_chunks = []
            for c in range(W // ck):
                s = _dot_nt(q2, k_ref[c * ck:(c + 1) * ck, sl].astype(BF16)) * (HEAD_DIM ** -0.5)
                ok = allowed[:, c:c + 1]
                if c >= n_full:
                    ok = jnp.where((c * ck + col) <= qpos, ok, 0.0)
                s_chunks.append(jnp.where(ok > 0.0, s, NEG))
            a = _softmax_av(
                s_chunks, lambda p, c: _dot(p.astype(BF16), v_ref[c * ck:(c + 1) * ck, sl].astype(BF16)))
            outs.append(_pair_out(a, tq))
        o_ref[...] = jnp.concatenate(outs, axis=1)

    _by_extent(i, tq, T, wb, run)


def _moba_prompt(Y, B, T, qc, kc, vc):
    assert T % MOBA_BLOCK == 0
    n_blk = T // MOBA_BLOCK
    nbp = -(-n_blk // LANES) * LANES
    topb = max(1, min(MOBA_TOPK, n_blk - 1))
    km = _kmean(Y, B, T, kc, nbp)
    tq = wb = MOBA_BLOCK
    gw = PAIRS_PER_STEP * LANES
    kern = functools.partial(_moba_prompt_kernel, tq=tq, T=T, wb=wb, topb=topb)
    return _pair_call(kern, "moba_prompt", Y, B, T, tq, qc, kc, vc,
                      mid=[(km, pl.BlockSpec((1, nbp, gw), lambda b, p, i: (b, 0, p)))])


def _cross_kernel(q_ref, mk_ref, mv_ref, o_ref):
    for h in range(MEM_HEADS):
        sl = slice(h * MEM_HEAD_DIM, (h + 1) * MEM_HEAD_DIM)
        s = _dot_nt(q_ref[:, sl].astype(BF16), mk_ref[:, sl].astype(BF16)) * (MEM_HEAD_DIM ** -0.5)
        mx = jnp.max(s, axis=1, keepdims=True)
        p = jnp.exp(s - mx)
        l = jnp.sum(p, axis=1, keepdims=True)
        o_ref[:, sl] = _dot(p.astype(BF16), mv_ref[:, sl].astype(BF16)) / l


def _cross_rows_kernel(q_ref, mk_ref, mv_ref, o_ref):
    mk = mk_ref[...].astype(BF16)
    mv = mv_ref[...].astype(BF16)
    tq = q_ref.shape[0]
    qs = jnp.concatenate([q_ref[:, h * MEM_HEAD_DIM:(h + 1) * MEM_HEAD_DIM] for h in range(MEM_HEADS)],
                         axis=0).astype(BF16)
    shape = (MEM_HEADS * tq, mk.shape[0])
    s = _dot_nt(qs, mk) * (MEM_HEAD_DIM ** -0.5)
    s = jnp.where(_iota(shape, 1) % MEM_HEADS == _iota(shape, 0) // tq, s, NEG)
    p = jnp.exp(s - jnp.max(s, axis=1, keepdims=True))
    o = _dot(p.astype(BF16), mv) / jnp.sum(p, axis=1, keepdims=True)
    o_ref[...] = jnp.concatenate([o[h * tq:(h + 1) * tq] for h in range(MEM_HEADS)], axis=1)


def _cross_rows(q, row0, B, Tq, mk, mv, rows, blk_of):
    r0 = row0 // Tq
    assert row0 % Tq == 0
    return pl.pallas_call(
        _cross_rows_kernel,
        grid=(B,),
        in_specs=[
            pl.BlockSpec((Tq, SEC), lambda b: (r0 + b, 0)),
            pl.BlockSpec((rows, MEM_HEAD_DIM), lambda b: (blk_of(b), 0)),
            pl.BlockSpec((rows, MEM_HEAD_DIM), lambda b: (blk_of(b), 0)),
        ],
        out_specs=pl.BlockSpec((Tq, SEC), lambda b: (b, 0)),
        out_shape=jax.ShapeDtypeStruct((B * Tq, SEC), F32),
        compiler_params=_cparams("parallel"),
        name="cross_rows",
    )(q, mk, mv)


def _cross(q, row0, B, Tq, mk, mv, mt, k_map, v_map):
    tq = _pick(Tq, 256)
    nq = Tq // tq
    r0 = row0 // tq
    assert row0 % tq == 0
    return pl.pallas_call(
        _cross_kernel,
        grid=(B, nq),
        in_specs=[
            pl.BlockSpec((tq, SEC), lambda b, i: (r0 + b * nq + i, 0)),
            pl.BlockSpec((mt, SEC), k_map),
            pl.BlockSpec((mt, SEC), v_map),
        ],
        out_specs=pl.BlockSpec((tq, SEC), lambda b, i: (b * nq + i, 0)),
        out_shape=jax.ShapeDtypeStruct((B * Tq, SEC), F32),
        compiler_params=_cparams("parallel", "arbitrary"),
        name="cross",
    )(q, mk, mv)


def _pad_rows(x, rows):
    return jnp.concatenate([x, jnp.zeros((rows - x.shape[0], x.shape[1]), x.dtype)], axis=0)


def _dec_positions(R, Ts, page, n_pages, past_len):
    col = _iota((R, page), 1)
    qpos = past_len + _iota((R, page), 0) % Ts
    kpos = [c * page + col for c in range(n_pages)] + [past_len + col]
    exists = [None] * n_pages + [col < Ts]
    return qpos, kpos, exists


def _diag_out(acc, n_heads, Ts, step=1):
    head = _iota((Ts, SEC), 1) // HEAD_DIM
    out = jnp.zeros((Ts, SEC), F32)
    for h in range(n_heads):
        r = step * h * Ts
        out = jnp.where(head == h, acc[r:r + Ts], out)
    return out


class _PagedKV:
    def __init__(self, kn_ref, vn_ref, kp, vp, page):
        self.kp, self.vp, self.page, self.n = kp, vp, page, len(kp)
        self.kn = _pad_rows(kn_ref[...], page).astype(BF16)
        self.vn = _pad_rows(vn_ref[...], page).astype(BF16)

    def kt_all(self):
        return jnp.concatenate([r[...].reshape(SEC, self.page) for r in self.kp], axis=1)

    def scores(self, q_bf16, kt_bf16=None):
        kt = self.kt_all().astype(BF16) if kt_bf16 is None else kt_bf16
        return _dot(q_bf16, kt), _dot_nt(q_bf16, self.kn)

    def pv(self, p_past, p_new):
        vt = jnp.concatenate([r[...].reshape(SEC, self.page) for r in self.vp], axis=1).astype(BF16)
        return _dot_nt(p_past.astype(BF16), vt) + _dot(p_new.astype(BF16), self.vn)


def _softmax_past_new(s_past, s_new, kv):
    mx = jnp.maximum(jnp.max(s_past, axis=1, keepdims=True), jnp.max(s_new, axis=1, keepdims=True))
    p_past = jnp.exp(s_past - mx)
    p_new = jnp.exp(s_new - mx)
    l = jnp.sum(p_past, axis=1, keepdims=True) + jnp.sum(p_new, axis=1, keepdims=True)
    return kv.pv(p_past, p_new) / l


def _new_causal(R, Ts, page, strict=False):
    j = _iota((R, page), 1)
    t = _iota((R, page), 0) % Ts
    return (j < t) if strict else (j <= t)


def _sb_dec_kernel(pt_ref, q_ref, kn_ref, vn_ref, *rest, n_pages, Ts, page, past_len):
    kp, vp = rest[:n_pages], rest[n_pages:2 * n_pages]
    H = SEC // HEAD_DIM
    R = H * Ts
    assert page == LANES
    qbd = _stack_masked(q_ref[...], H, HEAD_DIM).astype(BF16)
    kv = _PagedKV(kn_ref, vn_ref, kp, vp, page)
    z_past, z_new = kv.scores(qbd)
    nc = n_pages + 1
    z = jnp.concatenate([z_past[:, c * page:(c + 1) * page] for c in range(n_pages)] + [z_new],
                        axis=0) * (HEAD_DIM ** -0.5)
    row = _iota((nc * R, page), 0)
    kpos = (row // R) * page + _iota((nc * R, page), 1)
    mask = kpos < past_len + row % Ts
    ls = _log_sigmoid(z)
    m = jnp.where(mask, ls - z, 0.0)
    upper = (_iota((page, page), 0) > _iota((page, page), 1)).astype(BF16)
    inner = _dot_hilo(m, upper)
    tot = jnp.sum(m, axis=1, keepdims=True)
    run = jnp.zeros((R, 1), F32)
    later = [None] * nc
    for c in range(nc - 1, -1, -1):
        later[c] = run
        run = run + tot[c * R:(c + 1) * R]
    w = jnp.where(mask, jnp.exp(ls + inner + jnp.concatenate(later, axis=0)), 0.0)
    w_past = jnp.concatenate([w[c * R:(c + 1) * R] for c in range(n_pages)], axis=1)
    return _diag_out(kv.pv(w_past, w[n_pages * R:]), H, Ts)


def _dsa_dec_kernel(pt_ref, q_ref, iq_ref, kn_ref, vn_ref, *rest, n_pages, Ts, page, past_len, ksel):
    kp, vp = rest[:n_pages], rest[n_pages:2 * n_pages]
    ip = rest[2 * n_pages:3 * n_pages]
    H = SEC // HEAD_DIM
    iq = iq_ref[...]
    ki_off = IDX_HEADS * IDX_DIM
    ki_past = jnp.concatenate([r[...] for r in ip], axis=1).astype(BF16)
    score = jnp.concatenate(
        [_idx_scores(iq, ki_past, transposed=True),
         _idx_scores(iq, _pad_rows(iq[:, ki_off:ki_off + IDX_DIM], page).astype(BF16))], axis=1)
    kcol = _iota(score.shape, 1)
    valid = (kcol <= past_len + _iota(score.shape, 0)) & (kcol < past_len + Ts)
    sel = jnp.concatenate([_topk_select(score, valid, ksel, digit_bits=4)] * H, axis=0)
    qbd = _stack_masked(q_ref[...], H, HEAD_DIM).astype(BF16)
    kv = _PagedKV(kn_ref, vn_ref, kp, vp, page)
    s_past, s_new = kv.scores(qbd)
    s_past = jnp.where(sel[:, :past_len] > 0.0, s_past * (HEAD_DIM ** -0.5), NEG)
    s_new = jnp.where(sel[:, past_len:] > 0.0, s_new * (HEAD_DIM ** -0.5), NEG)
    return _diag_out(_softmax_past_new(s_past, s_new, kv), H, Ts)


def _diff_dec_kernel(pt_ref, lam_ref, q_ref, kn_ref, vn_ref, gs_ref, bd_ref, *rest,
                     n_pages, Ts, page, past_len, out_scale):
    kp, vp = rest[:n_pages], rest[n_pages:2 * n_pages]
    G = SEC // DIFF_DIM
    R = G * Ts
    qbd = _stack_masked(q_ref[...], G, DIFF_DIM).astype(BF16)
    kv = _PagedKV(kn_ref, vn_ref, kp, vp, page)
    s_past, s_new = kv.scores(qbd)
    s_new = jnp.where(_new_causal(R, Ts, page), s_new * (DIFF_DIM ** -0.5), NEG)
    a = _softmax_past_new(s_past * (DIFF_DIM ** -0.5), s_new, kv)
    lam = lam_ref[...]
    d = jnp.concatenate([a[(2 * h) * Ts:(2 * h + 1) * Ts] - lam * a[(2 * h + 1) * Ts:(2 * h + 2) * Ts]
                         for h in range(G // 2)], axis=0)
    o = _diag_out(d, G // 2, Ts)
    ssq = _dot_hilo(o * o, bd_ref[...])
    return o * lax.rsqrt(ssq * (1.0 / HEAD_DIM) + EPS) * gs_ref[...] * out_scale


def _moba_dec_kernel(pt_ref, q_ref, kn_ref, vn_ref, *rest, n_pages, Ts, page, past_len, topb):
    kp, vp = rest[:n_pages], rest[n_pages:2 * n_pages]
    H = SEC // HEAD_DIM
    R = H * Ts
    ppb = MOBA_BLOCK // page
    own = past_len // MOBA_BLOCK
    qbdf = _stack_masked(q_ref[...], H, HEAD_DIM)
    qbd = qbdf.astype(BF16)
    kv = _PagedKV(kn_ref, vn_ref, kp, vp, page)
    kt = kv.kt_all()
    nb_lane = _iota((SEC, LANES), 1)
    kmean_t = jnp.zeros((SEC, LANES), F32)
    for n in range(own):
        blk = functools.reduce(
            jnp.add, [kt[:, c * page:(c + 1) * page] for c in range(n * ppb, (n + 1) * ppb)])
        kmean_t = jnp.where(nb_lane == n, jnp.sum(blk, axis=1, keepdims=True) * (1.0 / MOBA_BLOCK), kmean_t)
    gate = _dot_hilo2(qbdf, kmean_t)
    sel = _top_blocks(gate, _iota((R, LANES), 1) < own, topb)
    s_past, s_new = kv.scores(qbd, kt.astype(BF16))
    picked = jnp.concatenate(
        [jnp.broadcast_to(sel[:, b:b + 1], (R, MOBA_BLOCK)) for b in range(own)], axis=1)
    s_past = jnp.where(picked > 0.0, s_past * (HEAD_DIM ** -0.5), NEG)
    s_new = jnp.where(_new_causal(R, Ts, page), s_new * (HEAD_DIM ** -0.5), NEG)
    return _diag_out(_softmax_past_new(s_past, s_new, kv), H, Ts)


DEC_BATCH_PER_STEP = 2


def _dec_multi(pt_ref, *refs, kern, nb, n_lead, n_row, n_tail, per, Ts):
    lead = refs[:n_lead]
    rows = refs[n_lead:n_lead + n_row]
    base = n_lead + n_row + n_tail
    tail = refs[n_lead + n_row:base]
    o_ref = refs[base + nb * per]
    outs = []
    for s in range(nb):
        sl = pl.ds(s * Ts, Ts)
        outs.append(kern(pt_ref, *lead, *[r.at[sl, :] for r in rows], *tail,
                         *refs[base + s * per:base + (s + 1) * per]))
    o_ref[...] = jnp.concatenate(outs, axis=0)


def _dec_call(kern, name, page_table, Y, row0, Ts, cache_k, cache_v, layer, half, q_cols, extra=(),
              idx_cache=None, idx_layer=0):
    Bs, n_pages = page_table.shape
    page = cache_k.shape[4]
    hpb = SEC // HEAD_DIM
    nb = DEC_BATCH_PER_STEP if Bs % DEC_BATCH_PER_STEP == 0 and row0 % (DEC_BATCH_PER_STEP * Ts) == 0 else 1
    rb = row0 // (nb * Ts)

    def ysec(col):
        return pl.BlockSpec((nb * Ts, SEC), lambda g, pt, col=col: (rb + g, col))

    def pspec(s, p):
        return pl.BlockSpec((None, None, hpb, HEAD_DIM, page),
                            lambda g, pt, s=s, p=p: (pt[nb * g + s, p], layer, half, 0, 0))

    def ispec(s, p):
        return pl.BlockSpec((None, None, IDX_DIM, page),
                            lambda g, pt, s=s, p=p: (pt[nb * g + s, p], idx_layer, 0, 0))

    in_specs = []
    ops = []
    lead = [e for e in extra if e[0] == "lead"]
    tail = [e for e in extra if e[0] == "tail"]
    for _, arr, spec in lead:
        ops.append(arr)
        in_specs.append(spec)
    for col in q_cols:
        ops.append(Y)
        in_specs.append(ysec(col))
    for _, arr, spec in tail:
        ops.append(arr)
        in_specs.append(spec)
    for s in range(nb):
        for p in range(n_pages):
            ops.append(cache_k)
            in_specs.append(pspec(s, p))
        for p in range(n_pages):
            ops.append(cache_v)
            in_specs.append(pspec(s, p))
        if idx_cache is not None:
            for p in range(n_pages):
                ops.append(idx_cache)
                in_specs.append(ispec(s, p))
    per = n_pages * (2 if idx_cache is None else 3)
    multi = functools.partial(_dec_multi, kern=kern, nb=nb, n_lead=len(lead), n_row=len(q_cols),
                              n_tail=len(tail), per=per, Ts=Ts)
    return pl.pallas_call(
        multi,
        grid_spec=pltpu.PrefetchScalarGridSpec(
            num_scalar_prefetch=1,
            grid=(Bs // nb,),
            in_specs=in_specs,
            out_specs=pl.BlockSpec((nb * Ts, SEC), lambda g, pt: (g, 0)),
        ),
        out_shape=jax.ShapeDtypeStruct((Bs * Ts, SEC), F32),
        compiler_params=_cparams("arbitrary"),
        name=name,
    )(page_table, *ops)


def kernel(x_prompt, x_sample, mem_prompt, cache_k, cache_v, cache_idx_k, cache_mem_k, cache_mem_v, page_table, g_norm, w_ffn1_gu, w_ffn1_d, w_ffn2_gu, w_ffn2_d, w_in_even, g_qk_b, w_in_odd, g_qk_c, g_qk_d, g_sub_c, lambda_c, w_out, w_mem_q, w_mem_kv, w_mem_o, g_mem_qk):
    Bp, Tp, D = x_prompt.shape
    Bs, Ts, _ = x_sample.shape
    depth = g_norm.shape[0]
    n_phys, _, page, n_kv, hd = cache_k.shape
    n_pages = page_table.shape[1]
    past_len = n_pages * page
    mt = mem_prompt.shape[1]
    Mp, Ms = Bp * Tp, Bs * Ts
    assert D == 2 * SEC and n_kv * hd == D and hd == HEAD_DIM
    assert Mp % Ts == 0 and past_len % MOBA_BLOCK == 0 and MOBA_BLOCK % page == 0 and Ts <= page

    x = jnp.concatenate([x_prompt.reshape(Mp, D), x_sample.reshape(Ms, D)], axis=0)
    mem = mem_prompt.reshape(Bp * mt, D)
    pos = jnp.concatenate([jnp.tile(jnp.arange(Tp, dtype=I32), Bp),
                           jnp.tile(jnp.arange(Ts, dtype=I32) + past_len, Bs)])
    tab_a = _rope_tables(pos, HEAD_DIM, ROT_DIM)
    tab_b = _rope_tables(pos, DIFF_DIM, DIFF_ROT)
    ck = jnp.transpose(cache_k, (0, 1, 3, 4, 2))
    cv = jnp.transpose(cache_v, (0, 1, 3, 4, 2))
    cik = jnp.transpose(cache_idx_k, (0, 1, 3, 2))
    cmk = cache_mem_k.reshape(Bs * depth * mt * MEM_HEADS, MEM_HEAD_DIM)
    cmv = cache_mem_v.reshape(Bs * depth * mt * MEM_HEADS, MEM_HEAD_DIM)
    ones = jnp.ones((SEC,), F32)
    zeros = jnp.zeros((SEC,), F32)
    idx_w = IDX_HEADS * IDX_DIM + IDX_DIM + IDX_HEADS
    idx_on = (jnp.arange(SEC) < IDX_HEADS * IDX_DIM + IDX_DIM).astype(F32)
    ksel_s = min(DSA_TOPK, (past_len + Ts) // 4)
    topb_s = max(1, min(MOBA_TOPK, -(-(past_len + Ts) // MOBA_BLOCK) - 1))
    bd64 = _group_ones(HEAD_DIM)

    Ys, mkvs = [], []
    for l in range(depth):
        g = g_norm[l]
        x = _ffn(x, g[0], w_ffn1_gu[l].astype(BF16), w_ffn1_d[l].astype(BF16))
        l2 = l // 2
        if l % 2 == 0:
            w = w_in_even[l2]
            wcols = [w[:, 0:512], w[:, 1536:2048], w[:, 512:1024], w[:, 2048:2560], w[:, 1024:1536],
                     w[:, 2560:3072], jnp.pad(w[:, 3072:3072 + idx_w], ((0, 0), (0, SEC - idx_w)))]
            cfgs = [(0, -1), (HEAD_DIM, 0), (0, -1), (HEAD_DIM, 0), (0, -1), (0, -1), (0, 0)]
            gains = [ones, jnp.tile(g_qk_b[l2, 0], 8), ones, jnp.tile(g_qk_b[l2, 1], 8), ones, ones, ones]
            rmask = [zeros, ones, zeros, ones, zeros, zeros, idx_on]
            tabs, ropes = list(tab_a), [(HEAD_DIM, ROT_DIM // 2)]
        else:
            w = w_in_odd[l2]
            wcols = [w[:, 0:512], w[:, 1536:2048], w[:, 512:1024], w[:, 2048:2560], w[:, 1024:1536],
                     w[:, 2560:3072]]
            cfgs = [(DIFF_DIM, 1), (HEAD_DIM, 0), (DIFF_DIM, 1), (HEAD_DIM, 0), (0, -1), (0, -1)]
            gains = [jnp.tile(g_qk_c[l2, 0], 8), jnp.tile(g_qk_d[l2, 0], 8), jnp.tile(g_qk_c[l2, 1], 8),
                     jnp.tile(g_qk_d[l2, 1], 8), ones, ones]
            rmask = [ones, ones, ones, ones, zeros, zeros]
            tabs = list(tab_a) + list(tab_b)
            ropes = [(HEAD_DIM, ROT_DIM // 2), (DIFF_DIM, DIFF_ROT // 2)]
        Y = _proj(x, g[1], jnp.concatenate(wcols, axis=1).astype(BF16),
                  jnp.concatenate(gains).reshape(1, -1), jnp.concatenate(rmask).reshape(1, -1),
                  cfgs, tabs, ropes)
        Ys.append(Y)

        if l % 2 == 0:
            oa_p = _sb_prompt(Y, Bp, Tp, 0, 2, 4)
            ob_p = _dsa_prompt(Y, Bp, Tp, 1, 3, 5, 6)
            oa_s = _dec_call(
                functools.partial(_sb_dec_kernel, n_pages=n_pages, Ts=Ts, page=page, past_len=past_len),
                "sb_dec", page_table, Y, Mp, Ts, ck, cv, l, 0, [0, 2, 4])
            ob_s = _dec_call(
                functools.partial(_dsa_dec_kernel, n_pages=n_pages, Ts=Ts, page=page, past_len=past_len,
                                  ksel=ksel_s),
                "dsa_dec", page_table, Y, Mp, Ts, ck, cv, l, 1, [1, 6, 3, 5],
                idx_cache=cik, idx_layer=l2)
        else:
            lam_init = 0.8 - 0.6 * math.exp(-0.3 * l)
            lp = lambda_c[l2].astype(F32)
            lam = (jnp.exp(jnp.sum(lp[0] * lp[1])) - jnp.exp(jnp.sum(lp[2] * lp[3])) + lam_init).reshape(1, 1)
            oa_p = _diff_prompt(Y, Bp, Tp, 0, 2, 4, lam, g_sub_c[l2], lam_init)
            ob_p = _moba_prompt(Y, Bp, Tp, 1, 3, 5)
            one = pl.BlockSpec((1, 1), lambda b, pt: (0, 0))
            oa_s = _dec_call(
                functools.partial(_diff_dec_kernel, n_pages=n_pages, Ts=Ts, page=page, past_len=past_len,
                                  out_scale=1.0 - lam_init),
                "diff_dec", page_table, Y, Mp, Ts, ck, cv, l, 0, [0, 2, 4],
                extra=[("lead", lam, one),
                       ("tail", jnp.tile(g_sub_c[l2], 8).reshape(1, SEC),
                        pl.BlockSpec((1, SEC), lambda b, pt: (0, 0))),
                       ("tail", bd64, pl.BlockSpec((SEC, SEC), lambda b, pt: (0, 0)))])
            ob_s = _dec_call(
                functools.partial(_moba_dec_kernel, n_pages=n_pages, Ts=Ts, page=page, past_len=past_len,
                                  topb=topb_s),
                "moba_dec", page_table, Y, Mp, Ts, ck, cv, l, 1, [1, 3, 5])
        o = jnp.concatenate([jnp.concatenate([oa_p, ob_p], axis=1),
                             jnp.concatenate([oa_s, ob_s], axis=1)], axis=0)
        x = _matres(o, w_out[l].astype(BF16), x)

        g_mq = jnp.tile(g_mem_qk[l, 0], MEM_HEADS).reshape(1, SEC)
        qm = _proj(x, g[2], w_mem_q[l].astype(BF16), g_mq, jnp.zeros((1, SEC), F32),
                   [(MEM_HEAD_DIM, -1)], [], [])
        g_mk = jnp.concatenate([jnp.tile(g_mem_qk[l, 1], MEM_HEADS), ones]).reshape(1, 2 * SEC)
        mkv = _proj(mem, g[4], w_mem_kv[l].astype(BF16), g_mk, jnp.zeros((1, 2 * SEC), F32),
                    [(MEM_HEAD_DIM, -1), (0, -1)], [], [])
        mkvs.append(mkv)
        oc_p = _cross(qm, 0, Bp, Tp, mkv, mkv, mt, lambda b, i: (b, 0), lambda b, i: (b, 1))
        oc_s = _cross_rows(qm, Mp, Bs, Ts, cmk, cmv, mt * MEM_HEADS, lambda b, l=l: b * depth + l)
        x = _matres(jnp.concatenate([oc_p, oc_s], axis=0), w_mem_o[l].astype(BF16), x)
        x = _ffn(x, g[3], w_ffn2_gu[l].astype(BF16), w_ffn2_d[l].astype(BF16))

    def stack(rows0, rows1, col0, col1, shape, arrs):
        return jnp.stack([a[rows0:rows1, col0:col1].reshape(shape) for a in arrs], axis=1)

    ki0 = 6 * SEC + IDX_HEADS * IDX_DIM
    y_prompt = x[:Mp].reshape(Bp, Tp, D)
    y_sample = x[Mp:].reshape(Bs, Ts, D)
    k_prompt = stack(0, Mp, 2 * SEC, 4 * SEC, (Bp, Tp, n_kv, hd), Ys)
    v_prompt = stack(0, Mp, 4 * SEC, 6 * SEC, (Bp, Tp, n_kv, hd), Ys)
    idx_k_prompt = stack(0, Mp, ki0, ki0 + IDX_DIM, (Bp, Tp, IDX_DIM), Ys[0::2])
    mem_k_prompt = stack(0, Bp * mt, 0, SEC, (Bp, mt, MEM_HEADS, MEM_HEAD_DIM), mkvs)
    mem_v_prompt = stack(0, Bp * mt, SEC, 2 * SEC, (Bp, mt, MEM_HEADS, MEM_HEAD_DIM), mkvs)
    k_sample = stack(Mp, Mp + Ms, 2 * SEC, 4 * SEC, (Bs, Ts, n_kv, hd), Ys)
    v_sample = stack(Mp, Mp + Ms, 4 * SEC, 6 * SEC, (Bs, Ts, n_kv, hd), Ys)
    idx_k_sample = stack(Mp, Mp + Ms, ki0, ki0 + IDX_DIM, (Bs, Ts, IDX_DIM), Ys[0::2])
    return (y_prompt, y_sample, k_prompt, v_prompt, idx_k_prompt, mem_k_prompt, mem_v_prompt,
            k_sample, v_sample, idx_k_sample)
```

```python
import functools
import math

import jax
import jax.numpy as jnp
from jax import lax
from jax.experimental import pallas as pl
from jax.experimental.pallas import tpu as pltpu

F32 = jnp.float32
BF16 = jnp.bfloat16
I32 = jnp.int32

EPS = 1e-6
HEAD_DIM = 64
DIFF_DIM = 32
ROT_DIM = 16
DIFF_ROT = 8
IDX_HEADS = 4
IDX_DIM = 64
DSA_TOPK = 256
MOBA_BLOCK = 256
MOBA_TOPK = 3
ROPE_THETA = 500000.0
MEM_HEADS = 4
MEM_HEAD_DIM = 128

LANES = 128
SEC = 512
NEG = -1e30
VMEM_LIMIT = 56 * 1024 * 1024


def _cparams(*sem):
    return pltpu.CompilerParams(dimension_semantics=sem, vmem_limit_bytes=VMEM_LIMIT)


def _pick(n, target):
    t = min(n, target)
    while t > 8 and (n % t or t % 8):
        t -= 1
    assert n % t == 0, (n, target)
    return t


def _dot(a, b):
    return jnp.dot(a, b, preferred_element_type=F32)


def _dot_nt(a, b):
    return lax.dot_general(a, b, (((1,), (1,)), ((), ())), preferred_element_type=F32)


def _split(x):
    hi = x.astype(BF16)
    lo = (x - hi.astype(F32)).astype(BF16)
    return hi, lo


def _dot_hilo(x, w_bf16):
    hi, lo = _split(x)
    return _dot(hi, w_bf16) + _dot(lo, w_bf16)


def _dot_nt_hilo2(a, b):
    ah, al = _split(a)
    bh, bl = _split(b)
    return _dot_nt(ah, bh) + _dot_nt(ah, bl) + _dot_nt(al, bh)


def _dot_hilo2(a, b):
    ah, al = _split(a)
    bh, bl = _split(b)
    return _dot(ah, bh) + _dot(ah, bl) + _dot(al, bh)


def _iota(shape, dim):
    return lax.broadcasted_iota(I32, shape, dim)


def _rms_rows(x, g):
    return x * lax.rsqrt(jnp.mean(x * x, axis=-1, keepdims=True) + EPS) * g


def _ffn_kernel(x_ref, g_ref, wg_ref, wu_ref, wd_ref, o_ref, h_scr, acc_scr):
    f = pl.program_id(1)

    @pl.when(f == 0)
    def _():
        h_scr[...] = _rms_rows(x_ref[...], g_ref[...]).astype(BF16)
        acc_scr[...] = jnp.zeros_like(acc_scr)

    h = h_scr[...]
    g = _dot(h, wg_ref[...])
    u = _dot(h, wu_ref[...])
    a = (g * (1.0 / (1.0 + jnp.exp(-g)))) * u
    acc_scr[...] += _dot(a.astype(BF16), wd_ref[...])

    @pl.when(f == pl.num_programs(1) - 1)
    def _():
        o_ref[...] = x_ref[...] + 0.5 * acc_scr[...]


def _ffn(x, g, w_gu, w_d):
    M, D = x.shape
    F = w_d.shape[0]
    tm = _pick(M, 1024)
    tf = _pick(F, 256)
    nf = F // tf
    return pl.pallas_call(
        _ffn_kernel,
        grid=(M // tm, nf),
        in_specs=[
            pl.BlockSpec((tm, D), lambda i, f: (i, 0)),
            pl.BlockSpec((1, D), lambda i, f: (0, 0)),
            pl.BlockSpec((D, tf), lambda i, f: (0, f)),
            pl.BlockSpec((D, tf), lambda i, f: (0, nf + f)),
            pl.BlockSpec((tf, D), lambda i, f: (f, 0)),
        ],
        out_specs=pl.BlockSpec((tm, D), lambda i, f: (i, 0)),
        out_shape=jax.ShapeDtypeStruct((M, D), F32),
        scratch_shapes=[pltpu.VMEM((tm, D), BF16), pltpu.VMEM((tm, D), F32)],
        compiler_params=_cparams("parallel", "arbitrary"),
        name="ffn",
    )(x, g.reshape(1, D), w_gu, w_gu, w_d)


def _group_ones(gs):
    r = jnp.arange(SEC) // gs
    return (r[:, None] == r[None, :]).astype(BF16)


def _rope_tables(pos, gs, rot):
    half = rot // 2
    freq = ROPE_THETA ** (-jnp.arange(half, dtype=F32) * 2.0 / rot)
    ang = pos.astype(F32)[:, None] * freq[None, :]
    cos, sin = jnp.cos(ang), jnp.sin(ang)
    d = jnp.arange(LANES) % gs
    c_full = cos[:, d % half]
    s_full = sin[:, d % half]
    C = jnp.where(d[None, :] < rot, c_full, 1.0)
    S = jnp.where(d[None, :] < half, -s_full, jnp.where(d[None, :] < rot, s_full, 0.0))
    return C, S


def _proj_kernel(*refs, cfgs, ropes, group_sizes, tn):
    x_ref, g_ref, w_ref, gain_ref, rmask_ref = refs[:5]
    n_tab = 2 * len(ropes)
    tab_refs = refs[5:5 + n_tab]
    bd_refs = refs[5 + n_tab:5 + n_tab + len(group_sizes)]
    o_ref, h_scr = refs[5 + n_tab + len(group_sizes):]
    j = pl.program_id(1)

    @pl.when(j == 0)
    def _():
        h_scr[...] = _rms_rows(x_ref[...], g_ref[...]).astype(BF16)

    y = _dot(h_scr[...], w_ref[...])
    tm = y.shape[0]

    def epilogue(y, cfg):
        gs, rope = cfg
        if gs:
            bd = bd_refs[group_sizes.index(gs)][...]
            ssq = _dot_hilo(y * y, bd)
            y = y * lax.rsqrt(ssq * (1.0 / gs) + EPS) * gain_ref[...]
        if rope < 0:
            o_ref[...] = y
            return
        period, half = ropes[rope]
        C = tab_refs[2 * rope][...]
        S = tab_refs[2 * rope + 1][...]
        first = (_iota((tm, LANES), 1) % period) < half
        for c in range(tn // LANES):
            sl = slice(c * LANES, (c + 1) * LANES)
            yc = y[:, sl]
            on = rmask_ref[:, sl] > 0.0
            partner = jnp.where(first, pltpu.roll(yc, LANES - half, 1), pltpu.roll(yc, half, 1))
            o_ref[:, sl] = yc * jnp.where(on, C, 1.0) + partner * jnp.where(on, S, 0.0)

    for cfg in sorted(set(cfgs)):
        cond = None
        for s, c in enumerate(cfgs):
            if c == cfg:
                cond = (j == s) if cond is None else (cond | (j == s))
        pl.when(cond)(functools.partial(epilogue, y, cfg))


def _proj(x, gnorm, w, gain, rmask, cfgs, tabs, ropes):
    M, D = x.shape
    N = w.shape[1]
    nsec = N // SEC
    assert nsec == len(cfgs)
    tm = _pick(M, 512)
    group_sizes = tuple(sorted({c[0] for c in cfgs if c[0]}))
    bds = [_group_ones(gs) for gs in group_sizes]
    kern = functools.partial(_proj_kernel, cfgs=tuple(cfgs), ropes=tuple(ropes),
                             group_sizes=group_sizes, tn=SEC)
    in_specs = [
        pl.BlockSpec((tm, D), lambda i, j: (i, 0)),
        pl.BlockSpec((1, D), lambda i, j: (0, 0)),
        pl.BlockSpec((D, SEC), lambda i, j: (0, j)),
        pl.BlockSpec((1, SEC), lambda i, j: (0, j)),
        pl.BlockSpec((1, SEC), lambda i, j: (0, j)),
    ]
    in_specs += [pl.BlockSpec((tm, LANES), lambda i, j: (i, 0)) for _ in tabs]
    in_specs += [pl.BlockSpec((SEC, SEC), lambda i, j: (0, 0)) for _ in bds]
    return pl.pallas_call(
        kern,
        grid=(M // tm, nsec),
        in_specs=in_specs,
        out_specs=pl.BlockSpec((tm, SEC), lambda i, j: (i, j)),
        out_shape=jax.ShapeDtypeStruct((M, N), F32),
        scratch_shapes=[pltpu.VMEM((tm, D), BF16)],
        compiler_params=_cparams("parallel", "arbitrary"),
        name="proj",
    )(x, gnorm.reshape(1, D), w, gain, rmask, *tabs, *bds)


def _matres_kernel(*refs, n_parts, n_prompt_tiles):
    pp, sp = refs[:n_parts], refs[n_parts:2 * n_parts]
    w_ref, x_ref, o_ref = refs[2 * n_parts:]
    i = pl.program_id(0)

    def run(parts):
        acc = x_ref[...]
        k0 = 0
        for r in parts:
            k1 = k0 + r.shape[1]
            acc = acc + _dot(r[...].astype(BF16), w_ref[k0:k1, :])
            k0 = k1
        o_ref[...] = acc

    pl.when(i < n_prompt_tiles)(functools.partial(run, pp))
    pl.when(i >= n_prompt_tiles)(functools.partial(run, sp))


def _matres(parts_p, parts_s, w, x):
    M, N = x.shape
    Mp, Ms = parts_p[0].shape[0], parts_s[0].shape[0]
    assert Mp + Ms == M
    tm = _pick(math.gcd(Mp, Ms), 1024)
    n_p = Mp // tm
    in_specs = [pl.BlockSpec((tm, a.shape[1]), lambda i: (jnp.minimum(i, n_p - 1), 0)) for a in parts_p]
    in_specs += [pl.BlockSpec((tm, a.shape[1]), lambda i: (jnp.maximum(i - n_p, 0), 0)) for a in parts_s]
    in_specs += [pl.BlockSpec(w.shape, lambda i: (0, 0)), pl.BlockSpec((tm, N), lambda i: (i, 0))]
    return pl.pallas_call(
        functools.partial(_matres_kernel, n_parts=len(parts_p), n_prompt_tiles=n_p),
        grid=(M // tm,),
        in_specs=in_specs,
        out_specs=pl.BlockSpec((tm, N), lambda i: (i, 0)),
        out_shape=jax.ShapeDtypeStruct((M, N), F32),
        compiler_params=_cparams("arbitrary"),
        name="matres",
    )(*parts_p, *parts_s, w, x)


def _stack_masked(q, n_groups, width):
    t, L = q.shape
    lane_grp = _iota((t, L), 1) // width
    return jnp.concatenate([jnp.where(lane_grp == r, q, 0.0) for r in range(n_groups)], axis=0)


def _log_sigmoid(z):
    return jnp.minimum(z, 0.0) - jnp.log(1.0 + jnp.exp(-jnp.abs(z)))


def _topk_select(score, valid, ksel, digit_bits=1):
    R, L = score.shape
    score = jnp.where(valid, score, -jnp.inf)
    bits = lax.bitcast_convert_type(score, I32)
    key = jnp.where(bits < 0, bits ^ jnp.int32(0x7FFFFFFF), bits)
    kf = jnp.float32(ksel)

    def count_ge(t):
        return jnp.sum((key >= t).astype(F32), axis=1, keepdims=True)

    int_min = jnp.int32(-2 ** 31)
    thr = jnp.where(count_ge(jnp.zeros((R, 1), I32)) >= kf, jnp.int32(0), int_min)

    if digit_bits == 1:
        def body(it, thr):
            cand = thr | lax.shift_left(jnp.int32(1), 30 - it)
            return jnp.where(count_ge(cand) >= kf, cand, thr)

        thr = lax.fori_loop(0, 31, body, thr)
    else:
        hi = 31
        while hi > 0:
            nbits = min(digit_bits, hi)
            shift = hi - nbits
            cands = [thr | jnp.int32(d << shift) for d in range(1, 2 ** nbits)]
            ge = jnp.concatenate([(key >= c).astype(F32) for c in cands], axis=0)
            cnt = jnp.sum(ge, axis=1, keepdims=True)
            digit = functools.reduce(
                jnp.add, [jnp.where(cnt[j * R:(j + 1) * R] >= kf, 1, 0) for j in range(len(cands))])
            thr = thr | lax.shift_left(digit.astype(I32), shift)
            hi = shift
    need = kf - jnp.sum((key > thr).astype(F32), axis=1, keepdims=True)
    validf = jnp.where(valid, 1.0, 0.0)
    incl = (_iota((LANES, LANES), 0) <= _iota((LANES, LANES), 1)).astype(BF16)
    run = jnp.zeros((R, 1), F32)
    outs = []
    for c in range(L // LANES):
        sl = slice(c * LANES, (c + 1) * LANES)
        key_c = key[:, sl]
        eqf = jnp.where(key_c == thr, 1.0, 0.0)
        rank = _dot(eqf.astype(BF16), incl) + run
        run = run + jnp.sum(eqf, axis=1, keepdims=True)
        take = jnp.where(key_c > thr, 1.0, jnp.where(rank <= need, eqf, 0.0))
        outs.append(take * validf[:, sl])
    return jnp.concatenate(outs, axis=1)


def _top_blocks(gate, ok, topb):
    R, NB = gate.shape
    n = _iota((R, NB), 1)
    g = jnp.where(ok, gate, -jnp.inf)
    sel = jnp.zeros((R, NB), F32)
    for _ in range(topb):
        mx = jnp.max(g, axis=1, keepdims=True)
        idx = jnp.min(jnp.where(g == mx, n, NB), axis=1, keepdims=True)
        pick = n == jnp.where(mx > -jnp.inf, idx, -1)
        sel = jnp.where(pick, 1.0, sel)
        g = jnp.where(pick, -jnp.inf, g)
    return sel


def _pair_out(a, tq):
    lane = _iota((tq, LANES), 1)
    return jnp.where(lane < HEAD_DIM, a[:tq], a[tq:])


def _by_extent(i, tq, T, wb, fn):
    need = ((i + 1) * tq + wb - 1) // wb
    for v in range(1, T // wb + 1):
        pl.when(need == v)(functools.partial(fn, v * wb))


def _extent_step(T):
    return _pick(T, max(T // 8, 256))


PAIRS_PER_STEP = 4


def _row_pos(i, tq, rows):
    return i * tq + _iota((rows, 1), 0) % tq


def _pair_call(kern, name, Y, B, T, tq, qc, kc, vc, lead=(), mid=(), tail=()):
    nq = T // tq
    gw = PAIRS_PER_STEP * LANES
    ng = SEC // gw
    in_specs = [s for _, s in lead]
    in_specs.append(pl.BlockSpec((tq, gw), lambda b, p, i: (b * nq + i, qc * ng + p)))
    in_specs += [s for _, s in mid]
    in_specs.append(pl.BlockSpec((T, gw), lambda b, p, i: (b, kc * ng + p)))
    in_specs.append(pl.BlockSpec((T, gw), lambda b, p, i: (b, vc * ng + p)))
    in_specs += [s for _, s in tail]
    ops = [a for a, _ in lead] + [Y] + [a for a, _ in mid] + [Y, Y] + [a for a, _ in tail]
    return pl.pallas_call(
        kern,
        grid=(B, ng, nq),
        in_specs=in_specs,
        out_specs=pl.BlockSpec((tq, gw), lambda b, p, i: (b * nq + i, p)),
        out_shape=jax.ShapeDtypeStruct((B * T, SEC), F32),
        compiler_params=_cparams("parallel", "parallel", "arbitrary"),
        name=name,
    )(*ops)


def _sb_prompt_kernel(q_ref, k_ref, v_ref, o_ref, *, tq, T, wb, ck):
    i = pl.program_id(2)
    R = 2 * tq
    upper = (_iota((ck, ck), 0) > _iota((ck, ck), 1)).astype(BF16)
    qpos = _row_pos(i, tq, R)
    col = _iota((R, ck), 1)

    def run(W):
        nc = W // ck
        n_full = (W - wb) // ck
        outs = []
        for pp in range(PAIRS_PER_STEP):
            sl = slice(pp * LANES, (pp + 1) * LANES)
            q2 = _stack_masked(q_ref[:, sl], 2, HEAD_DIM).astype(BF16)
            ls, ms, masks = [], [], []
            for c in range(nc):
                z = _dot_nt(q2, k_ref[c * ck:(c + 1) * ck, sl].astype(BF16)) * (HEAD_DIM ** -0.5)
                l_c = _log_sigmoid(z)
                ls.append(l_c)
                if c < n_full:
                    ms.append(l_c - z)
                    masks.append(None)
                else:
                    mask = (c * ck + col) < qpos
                    ms.append(jnp.where(mask, l_c - z, 0.0))
                    masks.append(mask)
            carry = jnp.zeros((R, 1), F32)
            acc = jnp.zeros((R, LANES), F32)
            for c in range(nc - 1, -1, -1):
                suffix = _dot_hilo(ms[c], upper) + carry
                carry = carry + jnp.sum(ms[c], axis=1, keepdims=True)
                w = jnp.exp(ls[c] + suffix)
                if masks[c] is not None:
                    w = jnp.where(masks[c], w, 0.0)
                acc = acc + _dot(w.astype(BF16), v_ref[c * ck:(c + 1) * ck, sl].astype(BF16))
            outs.append(_pair_out(acc, tq))
        o_ref[...] = jnp.concatenate(outs, axis=1)

    _by_extent(i, tq, T, wb, run)


def _sb_prompt(Y, B, T, qc, kc, vc):
    tq = _pick(T, 128)
    wb = _extent_step(T)
    kern = functools.partial(_sb_prompt_kernel, tq=tq, T=T, wb=wb, ck=_pick(wb, 256))
    return _pair_call(kern, "sb_prompt", Y, B, T, tq, qc, kc, vc)


def _idx_scores(iq, ki_bf16, transposed=False):
    score = None
    w_off = IDX_HEADS * IDX_DIM + IDX_DIM
    for h in range(IDX_HEADS):
        qh = iq[:, h * IDX_DIM:(h + 1) * IDX_DIM].astype(BF16)
        s = _dot(qh, ki_bf16) if transposed else _dot_nt(qh, ki_bf16)
        term = jnp.maximum(s, 0.0) * iq[:, w_off + h:w_off + h + 1]
        score = term if score is None else score + term
    return score * (IDX_DIM ** -0.5)


def _dsa_prompt_kernel(q_ref, iq_ref, k_ref, v_ref, ik_ref, o_ref, *, tq, T, wb, ksel):
    i = pl.program_id(1)
    ki_off = IDX_HEADS * IDX_DIM

    def run(W):
        ki = ik_ref[0:W, ki_off:ki_off + IDX_DIM].astype(BF16)
        score = _idx_scores(iq_ref[...], ki)
        causal = _iota((tq, W), 1) <= _row_pos(i, tq, tq)
        sel = _topk_select(score, causal, ksel)
        sel2 = jnp.concatenate([sel, sel], axis=0)
        outs = []
        for p in range(SEC // LANES):
            sl = slice(p * LANES, (p + 1) * LANES)
            q2 = _stack_masked(q_ref[:, sl], 2, HEAD_DIM).astype(BF16)
            s_chunks = []
            for c in range(W // wb):
                ks = slice(c * wb, (c + 1) * wb)
                s = _dot_nt(q2, k_ref[ks, sl].astype(BF16)) * (HEAD_DIM ** -0.5)
                s_chunks.append(jnp.where(sel2[:, ks] > 0.0, s, NEG))
            o2 = _softmax_av(
                s_chunks, lambda pe, c: _dot(pe.astype(BF16), v_ref[c * wb:(c + 1) * wb, sl].astype(BF16)))
            outs.append(_pair_out(o2, tq))
        o_ref[...] = jnp.concatenate(outs, axis=1)

    _by_extent(i, tq, T, wb, run)


def _dsa_prompt(Y, B, T, qc, kc, vc, ic):
    tq = _pick(T, 128)
    nq = T // tq
    ksel = min(DSA_TOPK, T // 4)
    return pl.pallas_call(
        functools.partial(_dsa_prompt_kernel, tq=tq, T=T, wb=_extent_step(T), ksel=ksel),
        grid=(B, nq),
        in_specs=[
            pl.BlockSpec((tq, SEC), lambda b, i: (b * nq + i, qc)),
            pl.BlockSpec((tq, SEC), lambda b, i: (b * nq + i, ic)),
            pl.BlockSpec((T, SEC), lambda b, i: (b, kc)),
            pl.BlockSpec((T, SEC), lambda b, i: (b, vc)),
            pl.BlockSpec((T, SEC), lambda b, i: (b, ic)),
        ],
        out_specs=pl.BlockSpec((tq, SEC), lambda b, i: (b * nq + i, 0)),
        out_shape=jax.ShapeDtypeStruct((B * T, SEC), F32),
        compiler_params=_cparams("parallel", "arbitrary"),
        name="dsa_prompt",
    )(Y, Y, Y, Y, Y)


def _softmax_av(s_chunks, pv):
    mx = jnp.max(functools.reduce(jnp.maximum, s_chunks), axis=1, keepdims=True)
    lsum, acc = None, None
    for c, s in enumerate(s_chunks):
        p = jnp.exp(s - mx)
        a = pv(p, c)
        lsum = p if lsum is None else lsum + p
        acc = a if acc is None else acc + a
    return acc / jnp.sum(lsum, axis=1, keepdims=True)


def _diff_prompt_kernel(lam_ref, q_ref, k_ref, v_ref, gs_ref, o_ref, *, tq, T, wb, ck, out_scale):
    i = pl.program_id(2)
    R = 4 * tq
    qpos = _row_pos(i, tq, R)
    col = _iota((R, ck), 1)

    def run(W):
        n_full = (W - wb) // ck
        outs = []
        for pp in range(PAIRS_PER_STEP):
            sl = slice(pp * LANES, (pp + 1) * LANES)
            q4 = _stack_masked(q_ref[:, sl], 4, DIFF_DIM).astype(BF16)
            s_chunks = []
            for c in range(W // ck):
                s = _dot_nt(q4, k_ref[c * ck:(c + 1) * ck, sl].astype(BF16)) * (DIFF_DIM ** -0.5)
                s_chunks.append(s if c < n_full else jnp.where((c * ck + col) <= qpos, s, NEG))
            a = _softmax_av(
                s_chunks, lambda p, c: _dot(p.astype(BF16), v_ref[c * ck:(c + 1) * ck, sl].astype(BF16)))
            lam = lam_ref[...]
            low = _iota((tq, LANES), 1) < HEAD_DIM
            o = jnp.where(low, a[:tq] - lam * a[tq:2 * tq], a[2 * tq:3 * tq] - lam * a[3 * tq:])
            o2 = o * o
            s0 = jnp.sum(jnp.where(low, o2, 0.0), axis=1, keepdims=True)
            s1 = jnp.sum(jnp.where(low, 0.0, o2), axis=1, keepdims=True)
            ssq = jnp.where(low, s0, s1)
            outs.append(o * lax.rsqrt(ssq * (1.0 / HEAD_DIM) + EPS) * gs_ref[...] * out_scale)
        o_ref[...] = jnp.concatenate(outs, axis=1)

    _by_extent(i, tq, T, wb, run)


def _diff_prompt(Y, B, T, qc, kc, vc, lam, gsub, lam_init):
    tq = _pick(T, 128)
    wb = _extent_step(T)
    kern = functools.partial(_diff_prompt_kernel, tq=tq, T=T, wb=wb, ck=_pick(wb, 256),
                             out_scale=1.0 - lam_init)
    return _pair_call(
        kern, "diff_prompt", Y, B, T, tq, qc, kc, vc,
        lead=[(lam, pl.BlockSpec((1, 1), lambda b, p, i: (0, 0)))],
        tail=[(jnp.tile(gsub, 2).reshape(1, LANES), pl.BlockSpec((1, LANES), lambda b, p, i: (0, 0)))])


def _kmean_kernel(k_ref, o_ref):
    n = pl.program_id(1)

    @pl.when(n == 0)
    def _():
        o_ref[...] = jnp.zeros_like(o_ref)

    o_ref[0, pl.ds(n, 1), :] = jnp.sum(k_ref[...], axis=0, keepdims=True) * (1.0 / MOBA_BLOCK)


def _kmean(Y, B, T, kc, nbp):
    n_blk = T // MOBA_BLOCK
    return pl.pallas_call(
        _kmean_kernel,
        grid=(B, n_blk),
        in_specs=[pl.BlockSpec((MOBA_BLOCK, SEC), lambda b, n: (b * n_blk + n, kc))],
        out_specs=pl.BlockSpec((1, nbp, SEC), lambda b, n: (b, 0, 0)),
        out_shape=jax.ShapeDtypeStruct((B, nbp, SEC), F32),
        compiler_params=_cparams("parallel", "arbitrary"),
        name="kmean",
    )(Y)


def _moba_prompt_kernel(q_ref, km_ref, k_ref, v_ref, o_ref, *, tq, T, wb, topb):
    i = pl.program_id(2)
    R = 2 * tq
    ck = MOBA_BLOCK
    own = (i * tq) // MOBA_BLOCK
    qpos = _row_pos(i, tq, R)
    col = _iota((R, ck), 1)

    def run(W):
        n_full = (W - wb) // ck
        q2fs = [_stack_masked(q_ref[:, pp * LANES:(pp + 1) * LANES], 2, HEAD_DIM)
                for pp in range(PAIRS_PER_STEP)]
        gate = jnp.concatenate([_dot_nt_hilo2(q2fs[pp], km_ref[0, :, pp * LANES:(pp + 1) * LANES])
                                for pp in range(PAIRS_PER_STEP)], axis=0)
        n = _iota(gate.shape, 1)
        allowed_all = jnp.where(n == own, 1.0, _top_blocks(gate, n < own, topb))
        outs = []
        for pp in range(PAIRS_PER_STEP):
            sl = slice(pp * LANES, (pp + 1) * LANES)
            q2 = q2fs[pp].astype(BF16)
            allowed = allowed_all[pp * R:(pp + 1) * R]
            s_chunks = []
            for c in range(W // ck):
                s = _dot_nt(q2, k_ref[c * ck:(c + 1) * ck, sl].astype(BF16)) * (HEAD_DIM ** -0.5)
                ok = allowed[:, c:c + 1]
                if c >= n_full:
                    ok = jnp.where((c * ck + col) <= qpos, ok, 0.0)
                s_chunks.append(jnp.where(ok > 0.0, s, NEG))
            a = _softmax_av(
                s_chunks, lambda p, c: _dot(p.astype(BF16), v_ref[c * ck:(c + 1) * ck, sl].astype(BF16)))
            outs.append(_pair_out(a, tq))
        o_ref[...] = jnp.concatenate(outs, axis=1)

    _by_extent(i, tq, T, wb, run)


def _moba_prompt(Y, B, T, qc, kc, vc):
    assert T % MOBA_BLOCK == 0
    n_blk = T // MOBA_BLOCK
    nbp = -(-n_blk // LANES) * LANES
    topb = max(1, min(MOBA_TOPK, n_blk - 1))
    km = _kmean(Y, B, T, kc, nbp)
    tq = wb = MOBA_BLOCK
    gw = PAIRS_PER_STEP * LANES
    kern = functools.partial(_moba_prompt_kernel, tq=tq, T=T, wb=wb, topb=topb)
    return _pair_call(kern, "moba_prompt", Y, B, T, tq, qc, kc, vc,
                      mid=[(km, pl.BlockSpec((1, nbp, gw), lambda b, p, i: (b, 0, p)))])


def _cross_kernel(q_ref, mk_ref, mv_ref, o_ref):
    for h in range(MEM_HEADS):
        sl = slice(h * MEM_HEAD_DIM, (h + 1) * MEM_HEAD_DIM)
        s = _dot_nt(q_ref[:, sl].astype(BF16), mk_ref[:, sl].astype(BF16)) * (MEM_HEAD_DIM ** -0.5)
        mx = jnp.max(s, axis=1, keepdims=True)
        p = jnp.exp(s - mx)
        l = jnp.sum(p, axis=1, keepdims=True)
        o_ref[:, sl] = _dot(p.astype(BF16), mv_ref[:, sl].astype(BF16)) / l


def _cross_rows_kernel(q_ref, mk_ref, mv_ref, o_ref):
    mk = mk_ref[...].astype(BF16)
    mv = mv_ref[...].astype(BF16)
    tq = q_ref.shape[0]
    qs = jnp.concatenate([q_ref[:, h * MEM_HEAD_DIM:(h + 1) * MEM_HEAD_DIM] for h in range(MEM_HEADS)],
                         axis=0).astype(BF16)
    shape = (MEM_HEADS * tq, mk.shape[0])
    s = _dot_nt(qs, mk) * (MEM_HEAD_DIM ** -0.5)
    s = jnp.where(_iota(shape, 1) % MEM_HEADS == _iota(shape, 0) // tq, s, NEG)
    p = jnp.exp(s - jnp.max(s, axis=1, keepdims=True))
    o = _dot(p.astype(BF16), mv) / jnp.sum(p, axis=1, keepdims=True)
    o_ref[...] = jnp.concatenate([o[h * tq:(h + 1) * tq] for h in range(MEM_HEADS)], axis=1)


def _cross_rows(q, row0, B, Tq, mk, mv, rows, blk_of):
    r0 = row0 // Tq
    assert row0 % Tq == 0
    return pl.pallas_call(
        _cross_rows_kernel,
        grid=(B,),
        in_specs=[
            pl.BlockSpec((Tq, SEC), lambda b: (r0 + b, 0)),
            pl.BlockSpec((rows, MEM_HEAD_DIM), lambda b: (blk_of(b), 0)),
            pl.BlockSpec((rows, MEM_HEAD_DIM), lambda b: (blk_of(b), 0)),
        ],
        out_specs=pl.BlockSpec((Tq, SEC), lambda b: (b, 0)),
        out_shape=jax.ShapeDtypeStruct((B * Tq, SEC), F32),
        compiler_params=_cparams("parallel"),
        name="cross_rows",
    )(q, mk, mv)


def _cross(q, row0, B, Tq, mk, mv, mt, k_map, v_map):
    tq = _pick(Tq, 256)
    nq = Tq // tq
    r0 = row0 // tq
    assert row0 % tq == 0
    return pl.pallas_call(
        _cross_kernel,
        grid=(B, nq),
        in_specs=[
            pl.BlockSpec((tq, SEC), lambda b, i: (r0 + b * nq + i, 0)),
            pl.BlockSpec((mt, SEC), k_map),
            pl.BlockSpec((mt, SEC), v_map),
        ],
        out_specs=pl.BlockSpec((tq, SEC), lambda b, i: (b * nq + i, 0)),
        out_shape=jax.ShapeDtypeStruct((B * Tq, SEC), F32),
        compiler_params=_cparams("parallel", "arbitrary"),
        name="cross",
    )(q, mk, mv)


def _pad_rows(x, rows):
    return jnp.concatenate([x, jnp.zeros((rows - x.shape[0], x.shape[1]), x.dtype)], axis=0)


def _dec_positions(R, Ts, page, n_pages, past_len):
    col = _iota((R, page), 1)
    qpos = past_len + _iota((R, page), 0) % Ts
    kpos = [c * page + col for c in range(n_pages)] + [past_len + col]
    exists = [None] * n_pages + [col < Ts]
    return qpos, kpos, exists


def _diag_out(acc, n_heads, Ts, step=1):
    head = _iota((Ts, SEC), 1) // HEAD_DIM
    out = jnp.zeros((Ts, SEC), F32)
    for h in range(n_heads):
        r = step * h * Ts
        out = jnp.where(head == h, acc[r:r + Ts], out)
    return out


class _PagedKV:
    def __init__(self, kn_ref, vn_ref, kp, vp, page):
        self.kp, self.vp, self.page, self.n = kp, vp, page, len(kp)
        self.kn = _pad_rows(kn_ref[...], page).astype(BF16)
        self.vn = _pad_rows(vn_ref[...], page).astype(BF16)

    def kt_all(self):
        return jnp.concatenate([r[...].reshape(SEC, self.page) for r in self.kp], axis=1)

    def scores(self, q_bf16, kt_bf16=None):
        kt = self.kt_all().astype(BF16) if kt_bf16 is None else kt_bf16
        return _dot(q_bf16, kt), _dot_nt(q_bf16, self.kn)

    def pv(self, p_past, p_new):
        vt = jnp.concatenate([r[...].reshape(SEC, self.page) for r in self.vp], axis=1).astype(BF16)
        return _dot_nt(p_past.astype(BF16), vt) + _dot(p_new.astype(BF16), self.vn)


def _softmax_past_new(s_past, s_new, kv):
    mx = jnp.maximum(jnp.max(s_past, axis=1, keepdims=True), jnp.max(s_new, axis=1, keepdims=True))
    p_past = jnp.exp(s_past - mx)
    p_new = jnp.exp(s_new - mx)
    l = jnp.sum(p_past, axis=1, keepdims=True) + jnp.sum(p_new, axis=1, keepdims=True)
    return kv.pv(p_past, p_new) / l


def _new_causal(R, Ts, page, strict=False):
    j = _iota((R, page), 1)
    t = _iota((R, page), 0) % Ts
    return (j < t) if strict else (j <= t)


def _sb_dec_kernel(pt_ref, q_ref, kn_ref, vn_ref, *rest, n_pages, Ts, page, past_len):
    kp, vp = rest[:n_pages], rest[n_pages:2 * n_pages]
    H = SEC // HEAD_DIM
    R = H * Ts
    assert page == LANES
    qbd = _stack_masked(q_ref[...], H, HEAD_DIM).astype(BF16)
    kv = _PagedKV(kn_ref, vn_ref, kp, vp, page)
    z_past, z_new = kv.scores(qbd)
    nc = n_pages + 1
    z = jnp.concatenate([z_past[:, c * page:(c + 1) * page] for c in range(n_pages)] + [z_new],
                        axis=0) * (HEAD_DIM ** -0.5)
    row = _iota((nc * R, page), 0)
    kpos = (row // R) * page + _iota((nc * R, page), 1)
    mask = kpos < past_len + row % Ts
    ls = _log_sigmoid(z)
    m = jnp.where(mask, ls - z, 0.0)
    upper = (_iota((page, page), 0) > _iota((page, page), 1)).astype(BF16)
    inner = _dot_hilo(m, upper)
    tot = jnp.sum(m, axis=1, keepdims=True)
    run = jnp.zeros((R, 1), F32)
    later = [None] * nc
    for c in range(nc - 1, -1, -1):
        later[c] = run
        run = run + tot[c * R:(c + 1) * R]
    w = jnp.where(mask, jnp.exp(ls + inner + jnp.concatenate(later, axis=0)), 0.0)
    w_past = jnp.concatenate([w[c * R:(c + 1) * R] for c in range(n_pages)], axis=1)
    return _diag_out(kv.pv(w_past, w[n_pages * R:]), H, Ts)


def _dsa_dec_multi(pt_ref, q_ref, iq_ref, kn_ref, vn_ref, *rest, nb, n_pages, Ts, page, past_len, ksel):
    per = 3 * n_pages
    o_ref = rest[nb * per]
    H = SEC // HEAD_DIM
    ki_off = IDX_HEADS * IDX_DIM
    scores = []
    for s in range(nb):
        iq = iq_ref[s * Ts:(s + 1) * Ts, :]
        ip = rest[s * per + 2 * n_pages:s * per + 3 * n_pages]
        ki_past = jnp.concatenate([r[...] for r in ip], axis=1).astype(BF16)
        scores.append(jnp.concatenate(
            [_idx_scores(iq, ki_past, transposed=True),
             _idx_scores(iq, _pad_rows(iq[:, ki_off:ki_off + IDX_DIM], page).astype(BF16))], axis=1))
    score = jnp.concatenate(scores, axis=0)
    kcol = _iota(score.shape, 1)
    valid = (kcol <= past_len + _iota(score.shape, 0) % Ts) & (kcol < past_len + Ts)
    sel_all = _topk_select(score, valid, ksel, digit_bits=4)
    outs = []
    for s in range(nb):
        rows = pl.ds(s * Ts, Ts)
        kp = rest[s * per:s * per + n_pages]
        vp = rest[s * per + n_pages:s * per + 2 * n_pages]
        sel = jnp.concatenate([sel_all[s * Ts:(s + 1) * Ts]] * H, axis=0)
        qbd = _stack_masked(q_ref[s * Ts:(s + 1) * Ts, :], H, HEAD_DIM).astype(BF16)
        kv = _PagedKV(kn_ref.at[rows, :], vn_ref.at[rows, :], kp, vp, page)
        s_past, s_new = kv.scores(qbd)
        s_past = jnp.where(sel[:, :past_len] > 0.0, s_past * (HEAD_DIM ** -0.5), NEG)
        s_new = jnp.where(sel[:, past_len:] > 0.0, s_new * (HEAD_DIM ** -0.5), NEG)
        outs.append(_diag_out(_softmax_past_new(s_past, s_new, kv), H, Ts))
    o_ref[...] = jnp.concatenate(outs, axis=0)


def _diff_dec_kernel(pt_ref, lam_ref, q_ref, kn_ref, vn_ref, gs_ref, bd_ref, *rest,
                     n_pages, Ts, page, past_len, out_scale):
    kp, vp = rest[:n_pages], rest[n_pages:2 * n_pages]
    G = SEC // DIFF_DIM
    R = G * Ts
    qbd = _stack_masked(q_ref[...], G, DIFF_DIM).astype(BF16)
    kv = _PagedKV(kn_ref, vn_ref, kp, vp, page)
    s_past, s_new = kv.scores(qbd)
    s_new = jnp.where(_new_causal(R, Ts, page), s_new * (DIFF_DIM ** -0.5), NEG)
    a = _softmax_past_new(s_past * (DIFF_DIM ** -0.5), s_new, kv)
    lam = lam_ref[...]
    d = jnp.concatenate([a[(2 * h) * Ts:(2 * h + 1) * Ts] - lam * a[(2 * h + 1) * Ts:(2 * h + 2) * Ts]
                         for h in range(G // 2)], axis=0)
    o = _diag_out(d, G // 2, Ts)
    ssq = _dot_hilo(o * o, bd_ref[...])
    return o * lax.rsqrt(ssq * (1.0 / HEAD_DIM) + EPS) * gs_ref[...] * out_scale


def _moba_dec_kernel(pt_ref, q_ref, kn_ref, vn_ref, *rest, n_pages, Ts, page, past_len, topb):
    kp, vp = rest[:n_pages], rest[n_pages:2 * n_pages]
    H = SEC // HEAD_DIM
    R = H * Ts
    ppb = MOBA_BLOCK // page
    own = past_len // MOBA_BLOCK
    qbdf = _stack_masked(q_ref[...], H, HEAD_DIM)
    qbd = qbdf.astype(BF16)
    kv = _PagedKV(kn_ref, vn_ref, kp, vp, page)
    kt = kv.kt_all()
    nb_lane = _iota((SEC, LANES), 1)
    kmean_t = jnp.zeros((SEC, LANES), F32)
    for n in range(own):
        blk = functools.reduce(
            jnp.add, [kt[:, c * page:(c + 1) * page] for c in range(n * ppb, (n + 1) * ppb)])
        kmean_t = jnp.where(nb_lane == n, jnp.sum(blk, axis=1, keepdims=True) * (1.0 / MOBA_BLOCK), kmean_t)
    gate = _dot_hilo2(qbdf, kmean_t)
    sel = _top_blocks(gate, _iota((R, LANES), 1) < own, topb)
    s_past, s_new = kv.scores(qbd, kt.astype(BF16))
    picked = jnp.concatenate(
        [jnp.broadcast_to(sel[:, b:b + 1], (R, MOBA_BLOCK)) for b in range(own)], axis=1)
    s_past = jnp.where(picked > 0.0, s_past * (HEAD_DIM ** -0.5), NEG)
    s_new = jnp.where(_new_causal(R, Ts, page), s_new * (HEAD_DIM ** -0.5), NEG)
    return _diag_out(_softmax_past_new(s_past, s_new, kv), H, Ts)


DEC_BATCH_PER_STEP = 2


def _dec_multi(pt_ref, *refs, kern, nb, n_lead, n_row, n_tail, per, Ts):
    lead = refs[:n_lead]
    rows = refs[n_lead:n_lead + n_row]
    base = n_lead + n_row + n_tail
    tail = refs[n_lead + n_row:base]
    o_ref = refs[base + nb * per]
    outs = []
    for s in range(nb):
        sl = pl.ds(s * Ts, Ts)
        outs.append(kern(pt_ref, *lead, *[r.at[sl, :] for r in rows], *tail,
                         *refs[base + s * per:base + (s + 1) * per]))
    o_ref[...] = jnp.concatenate(outs, axis=0)


def _dec_call(kern, name, page_table, Y, row0, Ts, cache_k, cache_v, layer, half, q_cols, extra=(),
              idx_cache=None, idx_layer=0, multi_kernel=None):
    Bs, n_pages = page_table.shape
    page = cache_k.shape[4]
    hpb = SEC // HEAD_DIM
    nb = DEC_BATCH_PER_STEP if Bs % DEC_BATCH_PER_STEP == 0 and row0 % (DEC_BATCH_PER_STEP * Ts) == 0 else 1
    rb = row0 // (nb * Ts)

    def ysec(col):
        return pl.BlockSpec((nb * Ts, SEC), lambda g, pt, col=col: (rb + g, col))

    def pspec(s, p):
        return pl.BlockSpec((None, None, hpb, HEAD_DIM, page),
                            lambda g, pt, s=s, p=p: (pt[nb * g + s, p], layer, half, 0, 0))

    def ispec(s, p):
        return pl.BlockSpec((None, None, IDX_DIM, page),
                            lambda g, pt, s=s, p=p: (pt[nb * g + s, p], idx_layer, 0, 0))

    in_specs = []
    ops = []
    lead = [e for e in extra if e[0] == "lead"]
    tail = [e for e in extra if e[0] == "tail"]
    for _, arr, spec in lead:
        ops.append(arr)
        in_specs.append(spec)
    for col in q_cols:
        ops.append(Y)
        in_specs.append(ysec(col))
    for _, arr, spec in tail:
        ops.append(arr)
        in_specs.append(spec)
    for s in range(nb):
        for p in range(n_pages):
            ops.append(cache_k)
            in_specs.append(pspec(s, p))
        for p in range(n_pages):
            ops.append(cache_v)
            in_specs.append(pspec(s, p))
        if idx_cache is not None:
            for p in range(n_pages):
                ops.append(idx_cache)
                in_specs.append(ispec(s, p))
    per = n_pages * (2 if idx_cache is None else 3)
    if multi_kernel is not None:
        multi = functools.partial(multi_kernel, nb=nb)
    else:
        multi = functools.partial(_dec_multi, kern=kern, nb=nb, n_lead=len(lead), n_row=len(q_cols),
                                  n_tail=len(tail), per=per, Ts=Ts)
    return pl.pallas_call(
        multi,
        grid_spec=pltpu.PrefetchScalarGridSpec(
            num_scalar_prefetch=1,
            grid=(Bs // nb,),
            in_specs=in_specs,
            out_specs=pl.BlockSpec((nb * Ts, SEC), lambda g, pt: (g, 0)),
        ),
        out_shape=jax.ShapeDtypeStruct((Bs * Ts, SEC), F32),
        compiler_params=_cparams("arbitrary"),
        name=name,
    )(page_table, *ops)


def kernel(x_prompt, x_sample, mem_prompt, cache_k, cache_v, cache_idx_k, cache_mem_k, cache_mem_v, page_table, g_norm, w_ffn1_gu, w_ffn1_d, w_ffn2_gu, w_ffn2_d, w_in_even, g_qk_b, w_in_odd, g_qk_c, g_qk_d, g_sub_c, lambda_c, w_out, w_mem_q, w_mem_kv, w_mem_o, g_mem_qk):
    Bp, Tp, D = x_prompt.shape
    Bs, Ts, _ = x_sample.shape
    depth = g_norm.shape[0]
    n_phys, _, page, n_kv, hd = cache_k.shape
    n_pages = page_table.shape[1]
    past_len = n_pages * page
    mt = mem_prompt.shape[1]
    Mp, Ms = Bp * Tp, Bs * Ts
    assert D == 2 * SEC and n_kv * hd == D and hd == HEAD_DIM
    assert Mp % Ts == 0 and past_len % MOBA_BLOCK == 0 and MOBA_BLOCK % page == 0 and Ts <= page

    x = jnp.concatenate([x_prompt.reshape(Mp, D), x_sample.reshape(Ms, D)], axis=0)
    mem = mem_prompt.reshape(Bp * mt, D)
    pos = jnp.concatenate([jnp.tile(jnp.arange(Tp, dtype=I32), Bp),
                           jnp.tile(jnp.arange(Ts, dtype=I32) + past_len, Bs)])
    tab_a = _rope_tables(pos, HEAD_DIM, ROT_DIM)
    tab_b = _rope_tables(pos, DIFF_DIM, DIFF_ROT)
    ck = jnp.transpose(cache_k, (0, 1, 3, 4, 2))
    cv = jnp.transpose(cache_v, (0, 1, 3, 4, 2))
    cik = jnp.transpose(cache_idx_k, (0, 1, 3, 2))
    cmk = cache_mem_k.reshape(Bs * depth * mt * MEM_HEADS, MEM_HEAD_DIM)
    cmv = cache_mem_v.reshape(Bs * depth * mt * MEM_HEADS, MEM_HEAD_DIM)
    ones = jnp.ones((SEC,), F32)
    zeros = jnp.zeros((SEC,), F32)
    idx_w = IDX_HEADS * IDX_DIM + IDX_DIM + IDX_HEADS
    idx_on = (jnp.arange(SEC) < IDX_HEADS * IDX_DIM + IDX_DIM).astype(F32)
    ksel_s = min(DSA_TOPK, (past_len + Ts) // 4)
    topb_s = max(1, min(MOBA_TOPK, -(-(past_len + Ts) // MOBA_BLOCK) - 1))
    bd64 = _group_ones(HEAD_DIM)

    Ys, mkvs = [], []
    for l in range(depth):
        g = g_norm[l]
        x = _ffn(x, g[0], w_ffn1_gu[l].astype(BF16), w_ffn1_d[l].astype(BF16))
        l2 = l // 2
        if l % 2 == 0:
            w = w_in_even[l2]
            wcols = [w[:, 0:512], w[:, 1536:2048], w[:, 512:1024], w[:, 2048:2560], w[:, 1024:1536],
                     w[:, 2560:3072], jnp.pad(w[:, 3072:3072 + idx_w], ((0, 0), (0, SEC - idx_w)))]
            cfgs = [(0, -1), (HEAD_DIM, 0), (0, -1), (HEAD_DIM, 0), (0, -1), (0, -1), (0, 0)]
            gains = [ones, jnp.tile(g_qk_b[l2, 0], 8), ones, jnp.tile(g_qk_b[l2, 1], 8), ones, ones, ones]
            rmask = [zeros, ones, zeros, ones, zeros, zeros, idx_on]
            tabs, ropes = list(tab_a), [(HEAD_DIM, ROT_DIM // 2)]
        else:
            w = w_in_odd[l2]
            wcols = [w[:, 0:512], w[:, 1536:2048], w[:, 512:1024], w[:, 2048:2560], w[:, 1024:1536],
                     w[:, 2560:3072]]
            cfgs = [(DIFF_DIM, 1), (HEAD_DIM, 0), (DIFF_DIM, 1), (HEAD_DIM, 0), (0, -1), (0, -1)]
            gains = [jnp.tile(g_qk_c[l2, 0], 8), jnp.tile(g_qk_d[l2, 0], 8), jnp.tile(g_qk_c[l2, 1], 8),
                     jnp.tile(g_qk_d[l2, 1], 8), ones, ones]
            rmask = [ones, ones, ones, ones, zeros, zeros]
            tabs = list(tab_a) + list(tab_b)
            ropes = [(HEAD_DIM, ROT_DIM // 2), (DIFF_DIM, DIFF_ROT // 2)]
        Y = _proj(x, g[1], jnp.concatenate(wcols, axis=1).astype(BF16),
                  jnp.concatenate(gains).reshape(1, -1), jnp.concatenate(rmask).reshape(1, -1),
                  cfgs, tabs, ropes)
        Ys.append(Y)

        if l % 2 == 0:
            oa_p = _sb_prompt(Y, Bp, Tp, 0, 2, 4)
            ob_p = _dsa_prompt(Y, Bp, Tp, 1, 3, 5, 6)
            oa_s = _dec_call(
                functools.partial(_sb_dec_kernel, n_pages=n_pages, Ts=Ts, page=page, past_len=past_len),
                "sb_dec", page_table, Y, Mp, Ts, ck, cv, l, 0, [0, 2, 4])
            ob_s = _dec_call(
                None, "dsa_dec", page_table, Y, Mp, Ts, ck, cv, l, 1, [1, 6, 3, 5],
                idx_cache=cik, idx_layer=l2,
                multi_kernel=functools.partial(_dsa_dec_multi, n_pages=n_pages, Ts=Ts, page=page,
                                               past_len=past_len, ksel=ksel_s))
        else:
            lam_init = 0.8 - 0.6 * math.exp(-0.3 * l)
            lp = lambda_c[l2].astype(F32)
            lam = (jnp.exp(jnp.sum(lp[0] * lp[1])) - jnp.exp(jnp.sum(lp[2] * lp[3])) + lam_init).reshape(1, 1)
            oa_p = _diff_prompt(Y, Bp, Tp, 0, 2, 4, lam, g_sub_c[l2], lam_init)
            ob_p = _moba_prompt(Y, Bp, Tp, 1, 3, 5)
            one = pl.BlockSpec((1, 1), lambda b, pt: (0, 0))
            oa_s = _dec_call(
                functools.partial(_diff_dec_kernel, n_pages=n_pages, Ts=Ts, page=page, past_len=past_len,
                                  out_scale=1.0 - lam_init),
                "diff_dec", page_table, Y, Mp, Ts, ck, cv, l, 0, [0, 2, 4],
                extra=[("lead", lam, one),
                       ("tail", jnp.tile(g_sub_c[l2], 8).reshape(1, SEC),
                        pl.BlockSpec((1, SEC), lambda b, pt: (0, 0))),
                       ("tail", bd64, pl.BlockSpec((SEC, SEC), lambda b, pt: (0, 0)))])
            ob_s = _dec_call(
                functools.partial(_moba_dec_kernel, n_pages=n_pages, Ts=Ts, page=page, past_len=past_len,
                                  topb=topb_s),
                "moba_dec", page_table, Y, Mp, Ts, ck, cv, l, 1, [1, 3, 5])
        x = _matres([oa_p, ob_p], [oa_s, ob_s], w_out[l].astype(BF16), x)

        g_mq = jnp.tile(g_mem_qk[l, 0], MEM_HEADS).reshape(1, SEC)
        qm = _proj(x, g[2], w_mem_q[l].astype(BF16), g_mq, jnp.zeros((1, SEC), F32),
                   [(MEM_HEAD_DIM, -1)], [], [])
        g_mk = jnp.concatenate([jnp.tile(g_mem_qk[l, 1], MEM_HEADS), ones]).reshape(1, 2 * SEC)
        mkv = _proj(mem, g[4], w_mem_kv[l].astype(BF16), g_mk, jnp.zeros((1, 2 * SEC), F32),
                    [(MEM_HEAD_DIM, -1), (0, -1)], [], [])
        mkvs.append(mkv)
        oc_p = _cross(qm, 0, Bp, Tp, mkv, mkv, mt, lambda b, i: (b, 0), lambda b, i: (b, 1))
        oc_s = _cross_rows(qm, Mp, Bs, Ts, cmk, cmv, mt * MEM_HEADS, lambda b, l=l: b * depth + l)
        x = _matres([oc_p], [oc_s], w_mem_o[l].astype(BF16), x)
        x = _ffn(x, g[3], w_ffn2_gu[l].astype(BF16), w_ffn2_d[l].astype(BF16))

    def stack(rows0, rows1, col0, col1, shape, arrs):
        return jnp.stack([a[rows0:rows1, col0:col1].reshape(shape) for a in arrs], axis=1)

    ki0 = 6 * SEC + IDX_HEADS * IDX_DIM
    y_prompt = x[:Mp].reshape(Bp, Tp, D)
    y_sample = x[Mp:].reshape(Bs, Ts, D)
    k_prompt = stack(0, Mp, 2 * SEC, 4 * SEC, (Bp, Tp, n_kv, hd), Ys)
    v_prompt = stack(0, Mp, 4 * SEC, 6 * SEC, (Bp, Tp, n_kv, hd), Ys)
    idx_k_prompt = stack(0, Mp, ki0, ki0 + IDX_DIM, (Bp, Tp, IDX_DIM), Ys[0::2])
    mem_k_prompt = stack(0, Bp * mt, 0, SEC, (Bp, mt, MEM_HEADS, MEM_HEAD_DIM), mkvs)
    mem_v_prompt = stack(0, Bp * mt, SEC, 2 * SEC, (Bp, mt, MEM_HEADS, MEM_HEAD_DIM), mkvs)
    k_sample = stack(Mp, Mp + Ms, 2 * SEC, 4 * SEC, (Bs, Ts, n_kv, hd), Ys)
    v_sample = stack(Mp, Mp + Ms, 4 * SEC, 6 * SEC, (Bs, Ts, n_kv, hd), Ys)
    idx_k_sample = stack(Mp, Mp + Ms, ki0, ki0 + IDX_DIM, (Bs, Ts, IDX_DIM), Ys[0::2])
    return (y_prompt, y_sample, k_prompt, v_prompt, idx_k_prompt, mem_k_prompt, mem_v_prompt,
            k_sample, v_sample, idx_k_sample)
```
